```python
import math
import jax, jax.numpy as jnp
from jax import lax
import numpy as np

D_MODEL = 1024
BATCH = 2
SEQ = 8192
DEPTH = 4
DEC_BATCH = 4
DEC_SEQ = 4096
PAST_LEN = 128

N_EVEN = (DEPTH + 1) // 2
N_ODD = DEPTH // 2
D_FF = 2816
EPS = 1e-6
BLOCK = 128

D_CONV = 512
CONV_A_WIDTH = 3

MLA_HEADS = 8
QK_NOPE = 64
QK_ROPE = 32
V_DIM = 64
Q_RANK = 384
KV_RANK = 256
ROPE_THETA = 10000.0

EVEN_IN = 3 * D_CONV + Q_RANK + KV_RANK + QK_ROPE
EVEN_MIX = D_CONV + MLA_HEADS * V_DIM

D_RNN = 512
LRU_BLOCKS = 8
LRU_BW = D_RNN // LRU_BLOCKS
CONV_C_WIDTH = 4
LRU_C = 8.0

SWA_HEADS = 8
SWA_KV_HEADS = 2
SWA_HD = 64
WINDOW = 128

ODD_IN = 2 * D_RNN + (SWA_HEADS + 2 * SWA_KV_HEADS) * SWA_HD
ODD_MIX = D_RNN + SWA_HEADS * SWA_HD

kernel_name = "hybrid_bidir_conv_mla_rglru_swa_macaron"


def rmsnorm(x, g):
    xf = x.astype(jnp.float32)
    y = xf * lax.rsqrt(jnp.mean(xf * xf, axis=-1, keepdims=True) + EPS)
    return (y * g.astype(jnp.float32)).astype(x.dtype)


def swiglu(x, w_gate, w_up, w_down):
    return (jax.nn.silu(x @ w_gate) * (x @ w_up)) @ w_down


def dwconv(x, w, pad_left, pad_right):
    return lax.conv_general_dilated(
        x, w[:, None, :].astype(x.dtype), window_strides=(1,),
        padding=[(pad_left, pad_right)], dimension_numbers=('NWC', 'WIO', 'NWC'),
        feature_group_count=x.shape[-1])


def rope(x, positions):
    half = x.shape[-1] // 2
    inv = ROPE_THETA ** (-jnp.arange(half, dtype=jnp.float32) / half)
    ang = positions.astype(jnp.float32)[:, None] * inv[None, :]
    cos = jnp.cos(ang)[:, None, :]
    sin = jnp.sin(ang)[:, None, :]
    x1 = x[..., :half].astype(jnp.float32)
    x2 = x[..., half:].astype(jnp.float32)
    return jnp.concatenate([x1 * cos - x2 * sin, x1 * sin + x2 * cos], axis=-1).astype(x.dtype)


def mla(q_lat, kv_lat, k_rope_raw, q_norm, w_uq, kv_norm, w_ukv):
    B, S, _ = q_lat.shape
    pos = jnp.arange(S)
    q = (rmsnorm(q_lat, q_norm) @ w_uq).reshape(B, S, MLA_HEADS, QK_NOPE + QK_ROPE)
    q_nope, q_pe = q[..., :QK_NOPE], rope(q[..., QK_NOPE:], pos)
    kv = (rmsnorm(kv_lat, kv_norm) @ w_ukv).reshape(B, S, MLA_HEADS, QK_NOPE + V_DIM)
    k_nope, v = kv[..., :QK_NOPE], kv[..., QK_NOPE:]
    k_pe = rope(k_rope_raw[:, :, None, :], pos)[:, :, 0, :]
    scale = (QK_NOPE + QK_ROPE) ** -0.5
    nblk = S // BLOCK
    qn_b = q_nope.reshape(B, nblk, BLOCK, MLA_HEADS, QK_NOPE).transpose(1, 0, 2, 3, 4)
    qp_b = q_pe.reshape(B, nblk, BLOCK, MLA_HEADS, QK_ROPE).transpose(1, 0, 2, 3, 4)

    def attend(blk):
        qn, qp = blk
        s = jnp.einsum('bqhd,bkhd->bhqk', qn, k_nope) + jnp.einsum('bqhr,bkr->bhqk', qp, k_pe)
        p = jax.nn.softmax(s.astype(jnp.float32) * scale, axis=-1).astype(v.dtype)
        return jnp.einsum('bhqk,bkhd->bqhd', p, v)

    o = lax.map(attend, (qn_b, qp_b))
    return o.transpose(1, 0, 2, 3, 4).reshape(B, S, MLA_HEADS * V_DIM)


def rglru_direction(x, w_a, b_a, w_x, b_x, lam, reverse):
    B, S, _ = x.shape
    xb = x.reshape(B, S, LRU_BLOCKS, LRU_BW)
    r = jax.nn.sigmoid(jnp.einsum('bsnc,ncd->bsnd', xb, w_a).reshape(B, S, D_RNN) + b_a)
    i = jax.nn.sigmoid(jnp.einsum('bsnc,ncd->bsnd', xb, w_x).reshape(B, S, D_RNN) + b_x)
    log_a = -LRU_C * r.astype(jnp.float32) * jax.nn.softplus(-lam.astype(jnp.float32))
    a = jnp.exp(log_a)
    u = jnp.sqrt(-jnp.expm1(2.0 * log_a)) * (i * x).astype(jnp.float32)

    def combine(left, right):
        a_l, b_l = left
        a_r, b_r = right
        return a_l * a_r, a_r * b_l + b_r

    _, h = lax.associative_scan(combine, (a, u), axis=1, reverse=reverse)
    return h.astype(x.dtype)


def windowed_gqa(q, k, v, sink):
    B, S, _ = q.shape
    nblk = S // BLOCK
    G = SWA_HEADS // SWA_KV_HEADS
    qb = q.reshape(B, nblk, BLOCK, SWA_KV_HEADS, G, SWA_HD)
    pad = ((0, 0), (BLOCK, BLOCK), (0, 0), (0, 0))
    kp = jnp.pad(k.reshape(B, S, SWA_KV_HEADS, SWA_HD), pad).reshape(B, nblk + 2, BLOCK, SWA_KV_HEADS, SWA_HD)
    vp = jnp.pad(v.reshape(B, S, SWA_KV_HEADS, SWA_HD), pad).reshape(B, nblk + 2, BLOCK, SWA_KV_HEADS, SWA_HD)
    kw = jnp.concatenate([kp[:, :-2], kp[:, 1:-1], kp[:, 2:]], axis=2)
    vw = jnp.concatenate([vp[:, :-2], vp[:, 1:-1], vp[:, 2:]], axis=2)
    s = jnp.einsum('bnqkgd,bnckd->bnkgqc', qb, kw).astype(jnp.float32) * (SWA_HD ** -0.5)
    qi = jnp.arange(BLOCK)[:, None]
    ci = jnp.arange(3 * BLOCK)[None, :]
    rel = jnp.abs(ci - BLOCK - qi)
    key_pos = jnp.arange(nblk)[:, None] * BLOCK - BLOCK + jnp.arange(3 * BLOCK)[None, :]
    valid = (rel <= WINDOW)[None] & ((key_pos >= 0) & (key_pos < S))[:, None, :]
    slopes = (2.0 ** (-8.0 * jnp.arange(1, SWA_HEADS + 1, dtype=jnp.float32) / SWA_HEADS)).reshape(SWA_KV_HEADS, G)
    s = s - slopes[:, :, None, None] * rel.astype(jnp.float32)
    s = jnp.where(valid[None, :, None, None], s, -jnp.inf)
    sk = sink.astype(jnp.float32).reshape(SWA_KV_HEADS, G)[None, None, :, :, None, None]
    m = jnp.maximum(jnp.max(s, axis=-1, keepdims=True), sk)
    e = jnp.exp(s - m)
    p = e / (jnp.sum(e, axis=-1, keepdims=True) + jnp.exp(sk - m))
    o = jnp.einsum('bnkgqc,bnckd->bnqkgd', p.astype(v.dtype), vw)
    return o.reshape(B, S, SWA_HEADS * SWA_HD)


def even_mixer(h, w_in, conv_w, q_norm, w_uq, kv_norm, w_ukv, w_out):
    z = h @ w_in
    b_g, c_g, xa, q_lat, kv_lat, k_r = jnp.split(
        z, [D_CONV, 2 * D_CONV, 3 * D_CONV, 3 * D_CONV + Q_RANK, 3 * D_CONV + Q_RANK + KV_RANK], axis=-1)
    y_a = b_g * dwconv(c_g * xa, conv_w, 1, 1)
    y_b = mla(q_lat, kv_lat, k_r, q_norm, w_uq, kv_norm, w_ukv)
    return jnp.concatenate([y_a, y_b], axis=-1) @ w_out


def odd_mixer(h, w_in, conv_w, conv_b, w_a, b_a, w_x, b_x, lam, sink, w_out):
    z = h @ w_in
    qd = SWA_HEADS * SWA_HD
    kd = SWA_KV_HEADS * SWA_HD
    xr, gate, q, k, v = jnp.split(z, [D_RNN, 2 * D_RNN, 2 * D_RNN + qd, 2 * D_RNN + qd + kd], axis=-1)
    xr = dwconv(xr, conv_w, 2, 1) + conv_b
    hr = (rglru_direction(xr, w_a[0], b_a[0], w_x[0], b_x[0], lam[0], False)
          + rglru_direction(xr, w_a[1], b_a[1], w_x[1], b_x[1], lam[1], True))
    y_c = jax.nn.gelu(gate) * hr
    y_d = windowed_gqa(q, k, v, sink)
    return jnp.concatenate([y_c, y_d], axis=-1) @ w_out


def trunk(x, ffn_norm, ffn_w_gate, ffn_w_up, ffn_w_down, mix_norm,
          ev_w_in, ev_conv_w, mla_q_norm, mla_w_uq, mla_kv_norm, mla_w_ukv, ev_w_out,
          od_w_in, od_conv_w, od_conv_b, lru_w_a, lru_b_a, lru_w_x, lru_b_x, lru_lambda,
          swa_sink, od_w_out, final_norm):
    for l in range(DEPTH):
        j = l // 2
        x = x + 0.5 * swiglu(rmsnorm(x, ffn_norm[l, 0]), ffn_w_gate[l, 0], ffn_w_up[l, 0], ffn_w_down[l, 0])
        h = rmsnorm(x, mix_norm[l])
        if l % 2 == 0:
            x = x + even_mixer(h, ev_w_in[j], ev_conv_w[j], mla_q_norm[j], mla_w_uq[j],
                               mla_kv_norm[j], mla_w_ukv[j], ev_w_out[j])
        else:
            x = x + odd_mixer(h, od_w_in[j], od_conv_w[j], od_conv_b[j], lru_w_a[j], lru_b_a[j],
                              lru_w_x[j], lru_b_x[j], lru_lambda[j], swa_sink[j], od_w_out[j])
        x = x + 0.5 * swiglu(rmsnorm(x, ffn_norm[l, 1]), ffn_w_gate[l, 1], ffn_w_up[l, 1], ffn_w_down[l, 1])
    return rmsnorm(x, final_norm)


def setup_inputs(seed: int = 0) -> dict:
    key = jax.random.key(seed)
    ks = jax.random.split(key, 26)
    f32 = jnp.float32

    def nrm(k, shape, scale):
        return jax.random.normal(k, shape, f32) * scale

    def gain(k, shape):
        return 1.0 + 0.02 * jax.random.normal(k, shape, f32)

    u = jax.random.uniform(ks[21], (N_ODD, 2, D_RNN), f32, minval=0.9, maxval=0.999)
    p = u ** (1.0 / LRU_C)
    lam = jnp.log(p) - jnp.log1p(-p)
    return {
        "x_prompt": nrm(ks[0], (BATCH, SEQ, D_MODEL), 1.0),
        "x_sample": nrm(ks[1], (DEC_BATCH, DEC_SEQ, D_MODEL), 1.0),
        "ffn_norm": gain(ks[2], (DEPTH, 2, D_MODEL)),
        "ffn_w_gate": nrm(ks[3], (DEPTH, 2, D_MODEL, D_FF), D_MODEL ** -0.5),
        "ffn_w_up": nrm(ks[4], (DEPTH, 2, D_MODEL, D_FF), D_MODEL ** -0.5),
        "ffn_w_down": nrm(ks[5], (DEPTH, 2, D_FF, D_MODEL), D_FF ** -0.5),
        "mix_norm": gain(ks[6], (DEPTH, D_MODEL)),
        "ev_w_in": nrm(ks[7], (N_EVEN, D_MODEL, EVEN_IN), D_MODEL ** -0.5),
        "ev_conv_w": nrm(ks[8], (N_EVEN, CONV_A_WIDTH, D_CONV), CONV_A_WIDTH ** -0.5),
        "mla_q_norm": gain(ks[9], (N_EVEN, Q_RANK)),
        "mla_w_uq": nrm(ks[10], (N_EVEN, Q_RANK, MLA_HEADS * (QK_NOPE + QK_ROPE)), Q_RANK ** -0.5),
        "mla_kv_norm": gain(ks[11], (N_EVEN, KV_RANK)),
        "mla_w_ukv": nrm(ks[12], (N_EVEN, KV_RANK, MLA_HEADS * (QK_NOPE + V_DIM)), KV_RANK ** -0.5),
        "ev_w_out": nrm(ks[13], (N_EVEN, EVEN_MIX, D_MODEL), EVEN_MIX ** -0.5),
        "od_w_in": nrm(ks[14], (N_ODD, D_MODEL, ODD_IN), D_MODEL ** -0.5),
        "od_conv_w": nrm(ks[15], (N_ODD, CONV_C_WIDTH, D_RNN), CONV_C_WIDTH ** -0.5),
        "od_conv_b": nrm(ks[16], (N_ODD, D_RNN), 0.01),
        "lru_w_a": nrm(ks[17], (N_ODD, 2, LRU_BLOCKS, LRU_BW, LRU_BW), LRU_BW ** -0.5),
        "lru_b_a": nrm(ks[18], (N_ODD, 2, D_RNN), 0.01),
        "lru_w_x": nrm(ks[19], (N_ODD, 2, LRU_BLOCKS, LRU_BW, LRU_BW), LRU_BW ** -0.5),
        "lru_b_x": nrm(ks[20], (N_ODD, 2, D_RNN), 0.01),
        "lru_lambda": lam,
        "swa_sink": nrm(ks[22], (N_ODD, SWA_HEADS), 0.5),
        "od_w_out": nrm(ks[23], (N_ODD, ODD_MIX, D_MODEL), ODD_MIX ** -0.5),
        "final_norm": gain(ks[24], (D_MODEL,)),
    }


def reference(x_prompt, x_sample, ffn_norm, ffn_w_gate, ffn_w_up, ffn_w_down, mix_norm,
              ev_w_in, ev_conv_w, mla_q_norm, mla_w_uq, mla_kv_norm, mla_w_ukv, ev_w_out,
              od_w_in, od_conv_w, od_conv_b, lru_w_a, lru_b_a, lru_w_x, lru_b_x, lru_lambda,
              swa_sink, od_w_out, final_norm):
    y_prompt = trunk(x_prompt, ffn_norm, ffn_w_gate, ffn_w_up, ffn_w_down, mix_norm,
                     ev_w_in, ev_conv_w, mla_q_norm, mla_w_uq, mla_kv_norm, mla_w_ukv, ev_w_out,
                     od_w_in, od_conv_w, od_conv_b, lru_w_a, lru_b_a, lru_w_x, lru_b_x, lru_lambda,
                     swa_sink, od_w_out, final_norm)
    y_sample = trunk(x_sample, ffn_norm, ffn_w_gate, ffn_w_up, ffn_w_down, mix_norm,
                     ev_w_in, ev_conv_w, mla_q_norm, mla_w_uq, mla_kv_norm, mla_w_ukv, ev_w_out,
                     od_w_in, od_conv_w, od_conv_b, lru_w_a, lru_b_a, lru_w_x, lru_b_x, lru_lambda,
                     swa_sink, od_w_out, final_norm)
    return (y_prompt, y_sample)
```

```python
import functools

import jax
import jax.numpy as jnp
from jax import lax
from jax.experimental import pallas as pl
from jax.experimental.pallas import tpu as pltpu

F32 = jnp.float32
BF16 = jnp.bfloat16

EPS = 1e-6
D_MODEL = 1024
D_FF = 2816
BLOCK = 128
LANES = 128
SUBLANES = 8

D_CONV = 512
MLA_HEADS = 8
QK_NOPE = 64
QK_ROPE = 32
V_DIM = 64
Q_RANK = 384
KV_RANK = 256
ROPE_THETA = 10000.0

D_RNN = 512
LRU_BLOCKS = 8
LRU_BW = D_RNN // LRU_BLOCKS
LRU_C = 8.0

SWA_HEADS = 8
SWA_KV_HEADS = 2
SWA_HD = 64
SWA_GROUP = SWA_HEADS // SWA_KV_HEADS
WINDOW = 128

TM = 512
FF_CHUNK = 1408
TQ = 512
TK = 512
TS = 512
TW = 512
HALO = SUBLANES

VMEM_LIMIT = 56 * 1024 * 1024

EV_QLAT = 3 * D_CONV
EV_KVLAT = EV_QLAT + Q_RANK
EV_KR = EV_KVLAT + KV_RANK
EV_IN_EXT = EV_KR + 2 * LANES
OD_Q = 2 * D_RNN
OD_K = OD_Q + SWA_HEADS * LANES
OD_V = OD_K + SWA_KV_HEADS * LANES
OD_IN_EXT = OD_V + 2 * SWA_KV_HEADS * LANES


def _const_spec(shape):
    zeros = (0,) * len(shape)
    return pl.BlockSpec(shape, lambda *_: zeros, pipeline_mode=pl.Buffered(1))


def _params(*sem):
    return pltpu.CompilerParams(dimension_semantics=sem, vmem_limit_bytes=VMEM_LIMIT)


def _rms(x, g):
    ms = jnp.mean(x * x, axis=-1, keepdims=True)
    return x * lax.rsqrt(ms + EPS) * g


def _dot(a, b):
    return jnp.dot(a, b, preferred_element_type=F32)


def _dot_nt(a, b):
    return lax.dot_general(a, b, (((1,), (1,)), ((), ())), preferred_element_type=F32)


def _ffn_body(x_ref, g_ref, wg_ref, wu_ref, wd_ref, *rest, final):
    o_ref = rest[-1]
    x = x_ref[...]
    xn = _rms(x, g_ref[...]).astype(BF16)
    acc = None
    for c in range(D_FF // FF_CHUNK):
        sl = slice(c * FF_CHUNK, (c + 1) * FF_CHUNK)
        gate = _dot(xn, wg_ref[:, sl])
        up = _dot(xn, wu_ref[:, sl])
        h = (gate * jax.nn.sigmoid(gate) * up).astype(BF16)
        y = _dot(h, wd_ref[sl, :])
        acc = y if acc is None else acc + y
    out = x + 0.5 * acc
    if final:
        out = _rms(out, rest[0][...])
    o_ref[...] = out


def _ffn(x, g, wg, wu, wd, final_g=None):
    m = x.shape[0]
    row = pl.BlockSpec((TM, D_MODEL), lambda i: (i, 0))
    in_specs = [row, _const_spec((1, D_MODEL)), _const_spec((D_MODEL, D_FF)),
                _const_spec((D_MODEL, D_FF)), _const_spec((D_FF, D_MODEL))]
    args = [x, g, wg, wu, wd]
    if final_g is not None:
        in_specs.append(_const_spec((1, D_MODEL)))
        args.append(final_g)
    return pl.pallas_call(
        functools.partial(_ffn_body, final=final_g is not None),
        grid=(m // TM,), in_specs=in_specs, out_specs=row,
        out_shape=jax.ShapeDtypeStruct((m, D_MODEL), F32),
        compiler_params=_params("parallel"), name="ffn")(*args)


def _even_in_body(x_ref, g_ref, win_ref, qg_ref, kvg_ref, wqa_ref, wqb_ref, wuk_ref, wuv_ref, tab_ref,
                  bg_ref, u_ref, q_ref, k_ref, v_ref):
    xn = _rms(x_ref[...], g_ref[...]).astype(BF16)
    z = _dot(xn, win_ref[...])
    bg_ref[...] = z[:, :D_CONV]
    u_ref[...] = z[:, D_CONV:2 * D_CONV] * z[:, 2 * D_CONV:3 * D_CONV]
    qn = _rms(z[:, EV_QLAT:EV_KVLAT], qg_ref[...]).astype(BF16)
    kvn = _rms(z[:, EV_KVLAT:EV_KR], kvg_ref[...]).astype(BF16)
    cosq = tab_ref[:, 0:LANES]
    sinq = tab_ref[:, LANES:2 * LANES]
    cosk = tab_ref[:, 2 * LANES:3 * LANES]
    sink = tab_ref[:, 3 * LANES:4 * LANES]
    kpe = z[:, EV_KR:EV_KR + LANES] * cosk + z[:, EV_KR + LANES:EV_KR + 2 * LANES] * sink
    qa = _dot(qn, wqa_ref[...])
    qb = _dot(qn, wqb_ref[...])
    kn = _dot(kvn, wuk_ref[...])
    v_ref[...] = _dot(kvn, wuv_ref[...]).astype(BF16)
    for h in range(MLA_HEADS):
        sl = slice(h * LANES, (h + 1) * LANES)
        q_ref[:, sl] = (qa[:, sl] * cosq + qb[:, sl] * sinq).astype(BF16)
        k_ref[:, sl] = (kn[:, sl] + kpe).astype(BF16)


def _even_in(x, g, w, tab, seq):
    m = x.shape[0]
    hw = MLA_HEADS * LANES
    tiles_per_seq = seq // TM
    row = lambda n: pl.BlockSpec((TM, n), lambda i: (i, 0))
    in_specs = [row(D_MODEL), _const_spec((1, D_MODEL)), _const_spec((D_MODEL, EV_IN_EXT)),
                _const_spec((1, Q_RANK)), _const_spec((1, KV_RANK)),
                _const_spec((Q_RANK, hw)), _const_spec((Q_RANK, hw)),
                _const_spec((KV_RANK, hw)), _const_spec((KV_RANK, hw)),
                pl.BlockSpec((TM, 4 * LANES), lambda i: (i % tiles_per_seq, 0))]
    out_specs = [row(D_CONV), row(D_CONV), row(hw), row(hw), row(hw)]
    out_shape = [jax.ShapeDtypeStruct((m, D_CONV), F32), jax.ShapeDtypeStruct((m, D_CONV), F32),
                 jax.ShapeDtypeStruct((m, hw), BF16), jax.ShapeDtypeStruct((m, hw), BF16),
                 jax.ShapeDtypeStruct((m, hw), BF16)]
    return pl.pallas_call(
        _even_in_body, grid=(m // TM,), in_specs=in_specs, out_specs=out_specs, out_shape=out_shape,
        compiler_params=_params("parallel"), name="even_in")(
            x, g, w["win"], w["q_norm"], w["kv_norm"], w["wqa"], w["wqb"], w["wuk"], w["wuv"], tab)


def _flash_body(q_ref, k_ref, v_ref, o_ref, m_ref, l_ref, acc_ref):
    j = pl.program_id(2)

    @pl.when(j == 0)
    def _():
        m_ref[...] = jnp.full(m_ref.shape, -jnp.inf, F32)
        l_ref[...] = jnp.zeros(l_ref.shape, F32)
        acc_ref[...] = jnp.zeros(acc_ref.shape, F32)

    low_half = lax.broadcasted_iota(jnp.int32, (TQ, LANES), 1) < V_DIM
    for h in range(MLA_HEADS):
        sl = slice(h * LANES, (h + 1) * LANES)
        s = _dot_nt(q_ref[:, sl], k_ref[:, sl])
        m_prev = m_ref[h]
        m_new = jnp.maximum(m_prev, jnp.max(s, axis=-1, keepdims=True))
        alpha = jnp.exp(m_prev - m_new)
        p = jnp.exp(s - m_new)
        l_ref[h] = alpha * l_ref[h] + jnp.sum(p, axis=-1, keepdims=True)
        m_ref[h] = m_new
        pv = _dot(p.astype(BF16), v_ref[:, sl])
        pr = slice((h // 2) * LANES, (h // 2 + 1) * LANES)
        mine = low_half if h % 2 == 0 else jnp.logical_not(low_half)
        acc_ref[:, pr] = acc_ref[:, pr] * jnp.where(mine, alpha, 1.0) + pv

    @pl.when(j == pl.num_programs(2) - 1)
    def _():
        for pair in range(MLA_HEADS // 2):
            pr = slice(pair * LANES, (pair + 1) * LANES)
            inv = jnp.where(low_half, 1.0 / l_ref[2 * pair], 1.0 / l_ref[2 * pair + 1])
            o_ref[:, pr] = (acc_ref[:, pr] * inv).astype(BF16)


def _flash(q, k, v, batch, seq):
    m = q.shape[0]
    hw = MLA_HEADS * LANES
    nq, nk = seq // TQ, seq // TK
    ow = MLA_HEADS * V_DIM
    return pl.pallas_call(
        _flash_body, grid=(batch, nq, nk),
        in_specs=[pl.BlockSpec((TQ, hw), lambda b, i, j: (b * nq + i, 0)),
                  pl.BlockSpec((TK, hw), lambda b, i, j: (b * nk + j, 0)),
                  pl.BlockSpec((TK, hw), lambda b, i, j: (b * nk + j, 0))],
        out_specs=pl.BlockSpec((TQ, ow), lambda b, i, j: (b * nq + i, 0)),
        out_shape=jax.ShapeDtypeStruct((m, ow), BF16),
        scratch_shapes=[pltpu.VMEM((MLA_HEADS, TQ, 1), F32), pltpu.VMEM((MLA_HEADS, TQ, 1), F32),
                        pltpu.VMEM((TQ, ow), F32)],
        compiler_params=_params("parallel", "parallel", "arbitrary"), name="mla_flash")(q, k, v)


def _even_out_body(bg_ref, u_ref, up_ref, un_ref, cw_ref, yb_ref, wo_ref, x_ref, o_ref, *, tiles_per_seq):
    t = pl.program_id(0) % tiles_per_seq
    u = u_ref[...]
    row = lax.broadcasted_iota(jnp.int32, u.shape, 0)
    prev = jnp.where(t == 0, 0.0, up_ref[HALO - 1:HALO, :])
    nxt = jnp.where(t == tiles_per_seq - 1, 0.0, un_ref[0:1, :])
    um1 = jnp.where(row == 0, prev, pltpu.roll(u, 1, 0))
    up1 = jnp.where(row == TM - 1, nxt, pltpu.roll(u, TM - 1, 0))
    conv = um1 * cw_ref[0:1, :] + u * cw_ref[1:2, :] + up1 * cw_ref[2:3, :]
    ya = (bg_ref[...] * conv).astype(BF16)
    y = _dot(ya, wo_ref[0:D_CONV, :]) + _dot(yb_ref[...], wo_ref[D_CONV:, :])
    o_ref[...] = x_ref[...] + y


def _halo_specs(tile, width, n_rows):
    per = tile // HALO
    last = n_rows // HALO - 1
    prev = pl.BlockSpec((HALO, width), lambda i: (jnp.maximum(i * per - 1, 0), 0))
    nxt = pl.BlockSpec((HALO, width), lambda i: (jnp.minimum((i + 1) * per, last), 0))
    return prev, nxt


def _even_out(bg, u, conv_w, yb, wo, x, seq):
    m = x.shape[0]
    row = lambda n: pl.BlockSpec((TM, n), lambda i: (i, 0))
    prev, nxt = _halo_specs(TM, D_CONV, m)
    return pl.pallas_call(
        functools.partial(_even_out_body, tiles_per_seq=seq // TM), grid=(m // TM,),
        in_specs=[row(D_CONV), row(D_CONV), prev, nxt, _const_spec(conv_w.shape), row(MLA_HEADS * V_DIM),
                  _const_spec(wo.shape), row(D_MODEL)],
        out_specs=row(D_MODEL), out_shape=jax.ShapeDtypeStruct((m, D_MODEL), F32),
        compiler_params=_params("parallel"), name="even_out")(bg, u, u, u, conv_w, yb, wo, x)


def _odd_in_body(x_ref, g_ref, win_ref, xr_ref, gate_ref, q_ref, k_ref, v_ref):
    xn = _rms(x_ref[...], g_ref[...]).astype(BF16)
    z = _dot(xn, win_ref[...])
    xr_ref[...] = z[:, :D_RNN]
    gate_ref[...] = z[:, D_RNN:OD_Q]
    q_ref[...] = (z[:, OD_Q:OD_K] * (SWA_HD ** -0.5)).astype(BF16)
    k_ref[...] = z[:, OD_K:OD_V].astype(BF16)
    v_ref[...] = z[:, OD_V:].astype(BF16)


def _odd_in(x, g, win):
    m = x.shape[0]
    row = lambda n: pl.BlockSpec((TM, n), lambda i: (i, 0))
    widths = [D_RNN, D_RNN, SWA_HEADS * LANES, SWA_KV_HEADS * LANES, 2 * SWA_KV_HEADS * LANES]
    dtypes = [F32, F32, BF16, BF16, BF16]
    return pl.pallas_call(
        _odd_in_body, grid=(m // TM,),
        in_specs=[row(D_MODEL), _const_spec((1, D_MODEL)), _const_spec((D_MODEL, OD_IN_EXT))],
        out_specs=[row(n) for n in widths],
        out_shape=[jax.ShapeDtypeStruct((m, n), d) for n, d in zip(widths, dtypes)],
        compiler_params=_params("parallel"), name="odd_in")(x, g, win)


def _lru_body(x_ref, xp_ref, xn_ref, cw_ref, cb_ref, wa_ref, ba_ref, wx_ref, bx_ref, lam_ref, o_ref,
              a_s, u_s, p_s, he_s, pe_s, c_s, carry_s, *, reverse, n_tiles):
    i = pl.program_id(1)
    t = (n_tiles - 1 - i) if reverse else i

    @pl.when(i == 0)
    def _():
        carry_s[...] = jnp.zeros(carry_s.shape, F32)

    x = x_ref[...]
    row = lax.broadcasted_iota(jnp.int32, x.shape, 0)
    pm1 = jnp.where(t == 0, 0.0, xp_ref[HALO - 1:HALO, :])
    pm2 = jnp.where(t == 0, 0.0, xp_ref[HALO - 2:HALO - 1, :])
    nx1 = jnp.where(t == n_tiles - 1, 0.0, xn_ref[0:1, :])
    xm1 = jnp.where(row == 0, pm1, pltpu.roll(x, 1, 0))
    xm2 = jnp.where(row == 0, pm2, jnp.where(row == 1, pm1, pltpu.roll(x, 2, 0)))
    xp1 = jnp.where(row == TS - 1, nx1, pltpu.roll(x, TS - 1, 0))
    xc = (xm2 * cw_ref[0:1, :] + xm1 * cw_ref[1:2, :] + x * cw_ref[2:3, :] + xp1 * cw_ref[3:4, :]
          + cb_ref[...])
    xcb = xc.astype(BF16)
    r = jax.nn.sigmoid(_dot(xcb, wa_ref[...]) + ba_ref[...])
    ig = jax.nn.sigmoid(_dot(xcb, wx_ref[...]) + bx_ref[...])
    nl = -lam_ref[...]
    softplus = jnp.maximum(nl, 0.0) + jnp.log1p(jnp.exp(-jnp.abs(nl)))
    log_a = -LRU_C * r * softplus
    a = jnp.exp(log_a)
    th = jnp.tanh(log_a)
    u = jnp.sqrt(-2.0 * th / (1.0 - th)) * (ig * xc)
    n_col = D_RNN // LANES
    for c in range(n_col):
        a_s[c] = a[:, c * LANES:(c + 1) * LANES]
        u_s[c] = u[:, c * LANES:(c + 1) * LANES]

    seg = TS // SUBLANES

    def rows(k):
        return pl.ds(k, SUBLANES, stride=seg)

    def local_scan(k, carry):
        kk = (seg - 1 - k) if reverse else k
        out = []
        for c in range(n_col):
            h, p = carry[c]
            ak = a_s[c, rows(kk), :]
            h = ak * h + u_s[c, rows(kk), :]
            p = ak * p
            u_s[c, rows(kk), :] = h
            p_s[c, rows(kk), :] = p
            out.append((h, p))
        return tuple(out)

    init = tuple((jnp.zeros((SUBLANES, LANES), F32), jnp.ones((SUBLANES, LANES), F32)) for _ in range(n_col))
    ends = lax.fori_loop(0, seg, local_scan, init)
    for c in range(n_col):
        he_s[:, c * LANES:(c + 1) * LANES] = ends[c][0]
        pe_s[:, c * LANES:(c + 1) * LANES] = ends[c][1]
    cr = carry_s[0:1, :]
    for j in (range(SUBLANES - 1, -1, -1) if reverse else range(SUBLANES)):
        c_s[j:j + 1, :] = cr
        cr = pe_s[j:j + 1, :] * cr + he_s[j:j + 1, :]
    carry_s[0:1, :] = cr

    def fixup(k, _):
        for c in range(n_col):
            u_s[c, rows(k), :] = u_s[c, rows(k), :] + p_s[c, rows(k), :] * c_s[:, c * LANES:(c + 1) * LANES]
        return 0

    lax.fori_loop(0, seg, fixup, 0)
    for c in range(n_col):
        o_ref[:, c * LANES:(c + 1) * LANES] = u_s[c]


def _lru(xr, w, d, batch, seq, reverse):
    m = xr.shape[0]
    n_tiles = seq // TS
    per = TS // HALO
    last = m // HALO - 1

    def tile(b, i):
        return b * n_tiles + ((n_tiles - 1 - i) if reverse else i)

    main = pl.BlockSpec((TS, D_RNN), lambda b, i: (tile(b, i), 0))
    prev = pl.BlockSpec((HALO, D_RNN), lambda b, i: (jnp.maximum(tile(b, i) * per - 1, 0), 0))
    nxt = pl.BlockSpec((HALO, D_RNN), lambda b, i: (jnp.minimum((tile(b, i) + 1) * per, last), 0))
    vec = _const_spec((1, D_RNN))
    mat = _const_spec((D_RNN, D_RNN))
    big = pltpu.VMEM((D_RNN // LANES, TS, LANES), F32)
    small = pltpu.VMEM((SUBLANES, D_RNN), F32)
    return pl.pallas_call(
        functools.partial(_lru_body, reverse=reverse, n_tiles=n_tiles), grid=(batch, n_tiles),
        in_specs=[main, prev, nxt, _const_spec(w["conv_w"].shape), vec, mat, vec, mat, vec, vec],
        out_specs=main, out_shape=jax.ShapeDtypeStruct((m, D_RNN), F32),
        scratch_shapes=[big, big, big, small, small, small, small],
        compiler_params=_params("arbitrary", "arbitrary"), name="lru_bwd" if reverse else "lru_fwd")(
            xr, xr, xr, w["conv_w"], w["conv_b"], w["wa"][d], w["ba"][d], w["wx"][d], w["bx"][d], w["lam"][d])


def _odd_out_body(sink_ref, q_ref, k_ref, kp_ref, kn_ref, v_ref, vp_ref, vn_ref, gate_ref, hf_ref, hb_ref,
                  wo_ref, x_ref, o_ref, kbuf, vbuf, yd_s, *, tiles_per_seq, seq):
    t = pl.program_id(0) % tiles_per_seq
    kbuf[0:BLOCK, :] = kp_ref[...]
    kbuf[BLOCK:BLOCK + TW, :] = k_ref[...]
    kbuf[BLOCK + TW:, :] = kn_ref[...]
    vbuf[0:BLOCK, :] = vp_ref[...]
    vbuf[BLOCK:BLOCK + TW, :] = v_ref[...]
    vbuf[BLOCK + TW:, :] = vn_ref[...]

    qi = lax.broadcasted_iota(jnp.int32, (BLOCK, 3 * BLOCK), 0)
    ci = lax.broadcasted_iota(jnp.int32, (BLOCK, 3 * BLOCK), 1)
    rel = jnp.abs(ci - BLOCK - qi)
    relf = rel.astype(F32)
    in_window = rel <= WINDOW
    for jb in range(TW // BLOCK):
        key_pos = t * TW + (jb - 1) * BLOCK + ci
        valid = in_window & (key_pos >= 0) & (key_pos < seq)
        ks = slice(jb * BLOCK, (jb + 3) * BLOCK)
        for pair in range(SWA_HEADS // 2):
            o = None
            for h in (2 * pair, 2 * pair + 1):
                g = h // SWA_GROUP
                qh = q_ref[jb * BLOCK:(jb + 1) * BLOCK, h * LANES:(h + 1) * LANES]
                s = _dot_nt(qh, kbuf[ks, g * LANES:(g + 1) * LANES])
                s = jnp.where(valid, s - (2.0 ** -(h + 1)) * relf, -jnp.inf)
                sk = sink_ref[h]
                mx = jnp.maximum(jnp.max(s, axis=-1, keepdims=True), sk)
                e = jnp.exp(s - mx)
                den = jnp.sum(e, axis=-1, keepdims=True) + jnp.exp(sk - mx)
                p = (e * (1.0 / den)).astype(BF16)
                vcol = 2 * g + (h % 2)
                oh = _dot(p, vbuf[ks, vcol * LANES:(vcol + 1) * LANES])
                o = oh if o is None else o + oh
            yd_s[jb * BLOCK:(jb + 1) * BLOCK, pair * LANES:(pair + 1) * LANES] = o.astype(BF16)

    yc = (jax.nn.gelu(gate_ref[...]) * (hf_ref[...] + hb_ref[...])).astype(BF16)
    y = _dot(yc, wo_ref[0:D_RNN, :]) + _dot(yd_s[...], wo_ref[D_RNN:, :])
    o_ref[...] = x_ref[...] + y


def _odd_out(sink, q, k, v, gate, hf, hb, wo, x, seq):
    m = x.shape[0]
    kw = SWA_KV_HEADS * LANES
    vw = 2 * SWA_KV_HEADS * LANES
    per = TW // BLOCK
    last = m // BLOCK - 1
    row = lambda n: pl.BlockSpec((TW, n), lambda i: (i, 0))
    prev = lambda n: pl.BlockSpec((BLOCK, n), lambda i: (jnp.maximum(i * per - 1, 0), 0))
    nxt = lambda n: pl.BlockSpec((BLOCK, n), lambda i: (jnp.minimum((i + 1) * per, last), 0))
    return pl.pallas_call(
        functools.partial(_odd_out_body, tiles_per_seq=seq // TW, seq=seq), grid=(m // TW,),
        in_specs=[pl.BlockSpec(memory_space=pltpu.SMEM),
                  row(SWA_HEADS * LANES), row(kw), prev(kw), nxt(kw), row(vw), prev(vw), nxt(vw),
                  row(D_RNN), row(D_RNN), row(D_RNN), _const_spec(wo.shape), row(D_MODEL)],
        out_specs=row(D_MODEL), out_shape=jax.ShapeDtypeStruct((m, D_MODEL), F32),
        scratch_shapes=[pltpu.VMEM((TW + 2 * BLOCK, kw), BF16), pltpu.VMEM((TW + 2 * BLOCK, vw), BF16),
                        pltpu.VMEM((TW, SWA_HEADS * SWA_HD), BF16)],
        compiler_params=_params("parallel"), name="odd_out")(
            sink, q, k, k, k, v, v, v, gate, hf, hb, wo, x)


def _rotate_half_cols(w):
    half = w.shape[-1] // 2
    return jnp.concatenate([-w[..., half:], w[..., :half]], axis=-1)


def _pair_pad(v, heads_axis_parity):
    z = jnp.zeros_like(v)
    return jnp.where(heads_axis_parity, jnp.concatenate([z, v], -1), jnp.concatenate([v, z], -1))


def _even_weights(w_in, conv_w, q_norm, w_uq, kv_norm, w_ukv, w_out):
    zc = lambda n: jnp.zeros((D_MODEL, n), F32)
    kr = w_in[:, EV_KR:]
    win = jnp.concatenate([w_in[:, :EV_KR], zc(QK_NOPE), kr, zc(LANES - QK_NOPE - QK_ROPE),
                           zc(QK_NOPE), _rotate_half_cols(kr), zc(LANES - QK_NOPE - QK_ROPE)], axis=1)
    wq = w_uq.reshape(Q_RANK, MLA_HEADS, QK_NOPE + QK_ROPE)
    nope, pe = wq[..., :QK_NOPE], wq[..., QK_NOPE:]
    zq = lambda n: jnp.zeros((Q_RANK, MLA_HEADS, n), F32)
    wqa = jnp.concatenate([nope, pe, zq(LANES - QK_NOPE - QK_ROPE)], -1).reshape(Q_RANK, MLA_HEADS * LANES)
    wqb = jnp.concatenate([zq(QK_NOPE), _rotate_half_cols(pe), zq(LANES - QK_NOPE - QK_ROPE)], -1)
    wqb = wqb.reshape(Q_RANK, MLA_HEADS * LANES)
    wkv = w_ukv.reshape(KV_RANK, MLA_HEADS, QK_NOPE + V_DIM)
    wuk = jnp.concatenate([wkv[..., :QK_NOPE], jnp.zeros((KV_RANK, MLA_HEADS, LANES - QK_NOPE), F32)], -1)
    odd_head = (jnp.arange(MLA_HEADS) % 2 == 1)[None, :, None]
    wuv = _pair_pad(wkv[..., QK_NOPE:], odd_head)
    return dict(win=win.astype(BF16), conv_w=conv_w, q_norm=q_norm[None, :], kv_norm=kv_norm[None, :],
                wqa=wqa.astype(BF16), wqb=wqb.astype(BF16),
                wuk=wuk.reshape(KV_RANK, MLA_HEADS * LANES).astype(BF16),
                wuv=wuv.reshape(KV_RANK, MLA_HEADS * LANES).astype(BF16), wo=w_out.astype(BF16))


def _block_diag(w):
    eye = jnp.eye(LRU_BLOCKS, dtype=w.dtype)
    return jnp.einsum('ncd,nm->ncmd', w, eye).reshape(D_RNN, D_RNN)


def _odd_weights(w_in, conv_w, conv_b, w_a, b_a, w_x, b_x, lam, sink, w_out):
    qd = SWA_HEADS * SWA_HD
    kd = SWA_KV_HEADS * SWA_HD
    pad = lambda w, heads: jnp.concatenate(
        [w.reshape(D_MODEL, heads, SWA_HD), jnp.zeros((D_MODEL, heads, LANES - SWA_HD), F32)], -1
    ).reshape(D_MODEL, heads * LANES)
    wq = w_in[:, OD_Q:OD_Q + qd]
    wk = w_in[:, OD_Q + qd:OD_Q + qd + kd]
    wv = w_in[:, OD_Q + qd + kd:].reshape(D_MODEL, SWA_KV_HEADS, 1, SWA_HD)
    wv = jnp.broadcast_to(wv, (D_MODEL, SWA_KV_HEADS, 2, SWA_HD))
    odd_slot = (jnp.arange(2) == 1)[None, None, :, None]
    wv = _pair_pad(wv, odd_slot).reshape(D_MODEL, 2 * SWA_KV_HEADS * LANES)
    win = jnp.concatenate([w_in[:, :OD_Q], pad(wq, SWA_HEADS), pad(wk, SWA_KV_HEADS), wv], axis=1)
    return dict(win=win.astype(BF16), conv_w=conv_w, conv_b=conv_b[None, :],
                wa=[_block_diag(w_a[d]).astype(BF16) for d in range(2)], ba=[b_a[d][None, :] for d in range(2)],
                wx=[_block_diag(w_x[d]).astype(BF16) for d in range(2)], bx=[b_x[d][None, :] for d in range(2)],
                lam=[lam[d][None, :] for d in range(2)], sink=sink, wo=w_out.astype(BF16))


def _rope_tables(seq):
    half = QK_ROPE // 2
    inv = ROPE_THETA ** (-jnp.arange(half, dtype=F32) / half)
    ang = jnp.arange(seq).astype(F32)[:, None] * inv[None, :]
    cos2 = jnp.tile(jnp.cos(ang), (1, 2))
    sin2 = jnp.tile(jnp.sin(ang), (1, 2))
    scale = (QK_NOPE + QK_ROPE) ** -0.5
    ones = jnp.ones((seq, QK_NOPE), F32)
    z_lo = jnp.zeros((seq, QK_NOPE), F32)
    z_hi = jnp.zeros((seq, LANES - QK_NOPE - QK_ROPE), F32)
    cosq = scale * jnp.concatenate([ones, cos2, z_hi], 1)
    sinq = scale * jnp.concatenate([z_lo, sin2, z_hi], 1)
    cosk = jnp.concatenate([z_lo, cos2, z_hi], 1)
    sink = jnp.concatenate([z_lo, sin2, z_hi], 1)
    return jnp.concatenate([cosq, sinq, cosk, sink], 1)


def _trunk(x3, ffn, mix_norm, even, odd, final_norm):
    batch, seq, _ = x3.shape
    x = x3.reshape(batch * seq, D_MODEL)
    tab = _rope_tables(seq)
    depth = len(ffn)
    for l in range(depth):
        x = _ffn(x, *ffn[l][0])
        g = mix_norm[l][None, :]
        if l % 2 == 0:
            w = even[l // 2]
            bg, u, q, k, v = _even_in(x, g, w, tab, seq)
            yb = _flash(q, k, v, batch, seq)
            x = _even_out(bg, u, w["conv_w"], yb, w["wo"], x, seq)
        else:
            w = odd[l // 2]
            xr, gate, q, k, v = _odd_in(x, g, w["win"])
            hf = _lru(xr, w, 0, batch, seq, reverse=False)
            hb = _lru(xr, w, 1, batch, seq, reverse=True)
            x = _odd_out(w["sink"], q, k, v, gate, hf, hb, w["wo"], x, seq)
        x = _ffn(x, *ffn[l][1], final_g=final_norm[None, :] if l == depth - 1 else None)
    return x.reshape(batch, seq, D_MODEL)


def kernel(x_prompt, x_sample, ffn_norm, ffn_w_gate, ffn_w_up, ffn_w_down, mix_norm, ev_w_in, ev_conv_w, mla_q_norm, mla_w_uq, mla_kv_norm, mla_w_ukv, ev_w_out, od_w_in, od_conv_w, od_conv_b, lru_w_a, lru_b_a, lru_w_x, lru_b_x, lru_lambda, swa_sink, od_w_out, final_norm):
    depth = ffn_norm.shape[0]
    ffn = [[(ffn_norm[l, s][None, :], ffn_w_gate[l, s].astype(BF16), ffn_w_up[l, s].astype(BF16),
             ffn_w_down[l, s].astype(BF16)) for s in range(2)] for l in range(depth)]
    even = [_even_weights(ev_w_in[j], ev_conv_w[j], mla_q_norm[j], mla_w_uq[j], mla_kv_norm[j], mla_w_ukv[j],
                          ev_w_out[j]) for j in range(ev_w_in.shape[0])]
    odd = [_odd_weights(od_w_in[j], od_conv_w[j], od_conv_b[j], lru_w_a[j], lru_b_a[j], lru_w_x[j], lru_b_x[j],
                        lru_lambda[j], swa_sink[j], od_w_out[j]) for j in range(od_w_in.shape[0])]
    y_prompt = _trunk(x_prompt, ffn, mix_norm, even, odd, final_norm)
    y_sample = _trunk(x_sample, ffn, mix_norm, even, odd, final_norm)
    return (y_prompt, y_sample)
```

```python
import functools
import math

import jax
import jax.numpy as jnp
from jax import lax
from jax.experimental import pallas as pl
from jax.experimental.pallas import tpu as pltpu

F32 = jnp.float32
BF16 = jnp.bfloat16

EPS = 1e-6
D_MODEL = 1024
D_FF = 2816
BLOCK = 128
LANES = 128
SUBLANES = 8

D_CONV = 512
MLA_HEADS = 8
QK_NOPE = 64
QK_ROPE = 32
V_DIM = 64
Q_RANK = 384
KV_RANK = 256
ROPE_THETA = 10000.0

D_RNN = 512
LRU_BLOCKS = 8
LRU_BW = D_RNN // LRU_BLOCKS
LRU_C = 8.0

SWA_HEADS = 8
SWA_KV_HEADS = 2
SWA_HD = 64
SWA_GROUP = SWA_HEADS // SWA_KV_HEADS
WINDOW = 128

TM = 512
FF_CHUNK = 1408
TQ = 1024
FLASH_UNIT = 512
TK = 1024
FLASH_ROWS = 64
TS = 512
TW = 512
HALO = SUBLANES

VMEM_LIMIT = 56 * 1024 * 1024

EV_QLAT = 3 * D_CONV
EV_KVLAT = EV_QLAT + Q_RANK
EV_KR = EV_KVLAT + KV_RANK
EV_IN_EXT = EV_KR + 2 * LANES
OD_Q = 2 * D_RNN
OD_K = OD_Q + SWA_HEADS * LANES
OD_V = OD_K + SWA_KV_HEADS * LANES
OD_IN_EXT = OD_V + SWA_KV_HEADS * LANES


def _const_spec(shape):
    zeros = (0,) * len(shape)
    return pl.BlockSpec(shape, lambda *_: zeros, pipeline_mode=pl.Buffered(1))


def _params(*sem):
    return pltpu.CompilerParams(dimension_semantics=sem, vmem_limit_bytes=VMEM_LIMIT)


def _rms(x, g):
    ms = jnp.mean(x * x, axis=-1, keepdims=True)
    return x * lax.rsqrt(ms + EPS) * g


def _dot(a, b):
    return jnp.dot(a, b, preferred_element_type=F32)


def _dot_nt(a, b):
    return lax.dot_general(a, b, (((1,), (1,)), ((), ())), preferred_element_type=F32)


def _ffn_body(x_ref, g_ref, wg_ref, wu_ref, wd_ref, *rest, final):
    o_ref = rest[-1]
    x = x_ref[...]
    xn = _rms(x, g_ref[...]).astype(BF16)
    acc = None
    for c in range(D_FF // FF_CHUNK):
        sl = slice(c * FF_CHUNK, (c + 1) * FF_CHUNK)
        gate = _dot(xn, wg_ref[:, sl])
        up = _dot(xn, wu_ref[:, sl])
        h = (gate * jax.nn.sigmoid(gate) * up).astype(BF16)
        y = _dot(h, wd_ref[sl, :])
        acc = y if acc is None else acc + y
    out = x + 0.5 * acc
    if final:
        out = _rms(out, rest[0][...])
    o_ref[...] = out


def _ffn(x, g, wg, wu, wd, final_g=None):
    m = x.shape[0]
    row = pl.BlockSpec((TM, D_MODEL), lambda i: (i, 0))
    in_specs = [row, _const_spec((1, D_MODEL)), _const_spec((D_MODEL, D_FF)),
                _const_spec((D_MODEL, D_FF)), _const_spec((D_FF, D_MODEL))]
    args = [x, g, wg, wu, wd]
    if final_g is not None:
        in_specs.append(_const_spec((1, D_MODEL)))
        args.append(final_g)
    return pl.pallas_call(
        functools.partial(_ffn_body, final=final_g is not None),
        grid=(m // TM,), in_specs=in_specs, out_specs=row,
        out_shape=jax.ShapeDtypeStruct((m, D_MODEL), F32),
        compiler_params=_params("parallel"), name="ffn")(*args)


def _even_in_body(x_ref, g_ref, win_ref, qg_ref, kvg_ref, wqa_ref, wqb_ref, wuk_ref, wuv_ref, tab_ref,
                  bg_ref, u_ref, q_ref, k_ref, v_ref):
    xn = _rms(x_ref[...], g_ref[...]).astype(BF16)
    z = _dot(xn, win_ref[...])
    bg_ref[...] = z[:, :D_CONV]
    u_ref[...] = z[:, D_CONV:2 * D_CONV] * z[:, 2 * D_CONV:3 * D_CONV]
    qn = _rms(z[:, EV_QLAT:EV_KVLAT], qg_ref[...]).astype(BF16)
    kvn = _rms(z[:, EV_KVLAT:EV_KR], kvg_ref[...]).astype(BF16)
    cosq = tab_ref[:, 0:LANES]
    sinq = tab_ref[:, LANES:2 * LANES]
    cosk = tab_ref[:, 2 * LANES:3 * LANES]
    sink = tab_ref[:, 3 * LANES:4 * LANES]
    kpe = z[:, EV_KR:EV_KR + LANES] * cosk + z[:, EV_KR + LANES:EV_KR + 2 * LANES] * sink
    qa = _dot(qn, wqa_ref[...])
    qb = _dot(qn, wqb_ref[...])
    kn = _dot(kvn, wuk_ref[...])
    lane = lax.broadcasted_iota(jnp.int32, (1, MLA_HEADS * LANES), 1)
    ones_lanes = jnp.where(lane % LANES >= V_DIM, 1.0, 0.0)
    v_ref[...] = (_dot(kvn, wuv_ref[...]) + ones_lanes).astype(BF16)
    for h in range(MLA_HEADS):
        sl = slice(h * LANES, (h + 1) * LANES)
        q_ref[:, sl] = (qa[:, sl] * cosq + qb[:, sl] * sinq).astype(BF16)
        k_ref[:, sl] = (kn[:, sl] + kpe).astype(BF16)


def _even_in(x, g, w, tab, seq):
    m = x.shape[0]
    hw = MLA_HEADS * LANES
    tiles_per_seq = seq // TM
    row = lambda n: pl.BlockSpec((TM, n), lambda i: (i, 0))
    in_specs = [row(D_MODEL), _const_spec((1, D_MODEL)), _const_spec((D_MODEL, EV_IN_EXT)),
                _const_spec((1, Q_RANK)), _const_spec((1, KV_RANK)),
                _const_spec((Q_RANK, hw)), _const_spec((Q_RANK, hw)),
                _const_spec((KV_RANK, hw)), _const_spec((KV_RANK, hw)),
                pl.BlockSpec((TM, 4 * LANES), lambda i: (i % tiles_per_seq, 0))]
    out_specs = [row(D_CONV), row(D_CONV), row(hw), row(hw), row(hw)]
    out_shape = [jax.ShapeDtypeStruct((m, D_CONV), F32), jax.ShapeDtypeStruct((m, D_CONV), F32),
                 jax.ShapeDtypeStruct((m, hw), BF16), jax.ShapeDtypeStruct((m, hw), BF16),
                 jax.ShapeDtypeStruct((m, hw), BF16)]
    return pl.pallas_call(
        _even_in_body, grid=(m // TM,), in_specs=in_specs, out_specs=out_specs, out_shape=out_shape,
        compiler_params=_params("parallel"), name="even_in")(
            x, g, w["win"], w["q_norm"], w["kv_norm"], w["wqa"], w["wqb"], w["wuk"], w["wuv"], tab)


def _flash_body(q_ref, k_ref, v_ref, o_ref, m_ref, acc_ref, s_buf, p_buf):
    j = pl.program_id(2)

    @pl.when(j == 0)
    def _():
        m_ref[...] = jnp.full(m_ref.shape, -jnp.inf, F32)
        acc_ref[...] = jnp.zeros(acc_ref.shape, F32)

    units = [(h, r0) for h in range(MLA_HEADS) for r0 in range(0, TQ, FLASH_UNIT)]

    def scores(u):
        h, r0 = units[u]
        sl = slice(h * LANES, (h + 1) * LANES)
        s_buf[u % 2] = _dot_nt(q_ref[r0:r0 + FLASH_UNIT, sl], k_ref[:, sl])

    scores(0)
    for u, (h, r0) in enumerate(units):
        if u + 1 < len(units):
            scores(u + 1)
        for r in range(0, FLASH_UNIT, FLASH_ROWS):
            rows = slice(r0 + r, r0 + r + FLASH_ROWS)
            s = s_buf[u % 2, r:r + FLASH_ROWS, :]
            m_prev = m_ref[h, rows, :]
            m_new = jnp.maximum(m_prev, jnp.max(s, axis=-1, keepdims=True))
            p_buf[u % 2, r:r + FLASH_ROWS, :] = jnp.exp2(s - m_new).astype(BF16)
            acc_ref[h, rows, :] = acc_ref[h, rows, :] * jnp.exp2(m_prev - m_new)
            m_ref[h, rows, :] = m_new
        acc_ref[h, r0:r0 + FLASH_UNIT, :] += _dot(p_buf[u % 2], v_ref[:, h * LANES:(h + 1) * LANES])

    @pl.when(j == pl.num_programs(2) - 1)
    def _():
        low_half = lax.broadcasted_iota(jnp.int32, (TQ, LANES), 1) < V_DIM
        for pair in range(MLA_HEADS // 2):
            even = acc_ref[2 * pair]
            odd = acc_ref[2 * pair + 1]
            even_sw = pltpu.roll(even, V_DIM, 1)
            odd_sw = pltpu.roll(odd, V_DIM, 1)
            out = jnp.where(low_half, even / even_sw, odd_sw / odd)
            o_ref[:, pair * LANES:(pair + 1) * LANES] = out.astype(BF16)


def _flash(q, k, v, batch, seq):
    m = q.shape[0]
    hw = MLA_HEADS * LANES
    nq, nk = seq // TQ, seq // TK
    ow = MLA_HEADS * V_DIM
    return pl.pallas_call(
        _flash_body, grid=(batch, nq, nk),
        in_specs=[pl.BlockSpec((TQ, hw), lambda b, i, j: (b * nq + i, 0)),
                  pl.BlockSpec((TK, hw), lambda b, i, j: (b * nk + j, 0)),
                  pl.BlockSpec((TK, hw), lambda b, i, j: (b * nk + j, 0))],
        out_specs=pl.BlockSpec((TQ, ow), lambda b, i, j: (b * nq + i, 0)),
        out_shape=jax.ShapeDtypeStruct((m, ow), BF16),
        scratch_shapes=[pltpu.VMEM((MLA_HEADS, TQ, 1), F32), pltpu.VMEM((MLA_HEADS, TQ, LANES), F32),
                        pltpu.VMEM((2, FLASH_UNIT, TK), F32), pltpu.VMEM((2, FLASH_UNIT, TK), BF16)],
        compiler_params=_params("parallel", "parallel", "arbitrary"), name="mla_flash")(q, k, v)


def _even_out_body(bg_ref, u_ref, up_ref, un_ref, cw_ref, yb_ref, wo_ref, x_ref, o_ref, *, tiles_per_seq):
    t = pl.program_id(0) % tiles_per_seq
    u = u_ref[...]
    row = lax.broadcasted_iota(jnp.int32, u.shape, 0)
    prev = jnp.where(t == 0, 0.0, up_ref[HALO - 1:HALO, :])
    nxt = jnp.where(t == tiles_per_seq - 1, 0.0, un_ref[0:1, :])
    um1 = jnp.where(row == 0, prev, pltpu.roll(u, 1, 0))
    up1 = jnp.where(row == TM - 1, nxt, pltpu.roll(u, TM - 1, 0))
    conv = um1 * cw_ref[0:1, :] + u * cw_ref[1:2, :] + up1 * cw_ref[2:3, :]
    ya = (bg_ref[...] * conv).astype(BF16)
    y = _dot(ya, wo_ref[0:D_CONV, :]) + _dot(yb_ref[...], wo_ref[D_CONV:, :])
    o_ref[...] = x_ref[...] + y


def _halo_specs(tile, width, n_rows):
    per = tile // HALO
    last = n_rows // HALO - 1
    prev = pl.BlockSpec((HALO, width), lambda i: (jnp.maximum(i * per - 1, 0), 0))
    nxt = pl.BlockSpec((HALO, width), lambda i: (jnp.minimum((i + 1) * per, last), 0))
    return prev, nxt


def _even_out(bg, u, conv_w, yb, wo, x, seq):
    m = x.shape[0]
    row = lambda n: pl.BlockSpec((TM, n), lambda i: (i, 0))
    prev, nxt = _halo_specs(TM, D_CONV, m)
    return pl.pallas_call(
        functools.partial(_even_out_body, tiles_per_seq=seq // TM), grid=(m // TM,),
        in_specs=[row(D_CONV), row(D_CONV), prev, nxt, _const_spec(conv_w.shape), row(MLA_HEADS * V_DIM),
                  _const_spec(wo.shape), row(D_MODEL)],
        out_specs=row(D_MODEL), out_shape=jax.ShapeDtypeStruct((m, D_MODEL), F32),
        compiler_params=_params("parallel"), name="even_out")(bg, u, u, u, conv_w, yb, wo, x)


def _odd_in_body(x_ref, g_ref, win_ref, xr_ref, gate_ref, q_ref, k_ref, v_ref):
    xn = _rms(x_ref[...], g_ref[...]).astype(BF16)
    z = _dot(xn, win_ref[...])
    xr_ref[...] = z[:, :D_RNN]
    gate_ref[...] = z[:, D_RNN:OD_Q]
    q_ref[...] = (z[:, OD_Q:OD_K] * (SWA_HD ** -0.5)).astype(BF16)
    k_ref[...] = z[:, OD_K:OD_V].astype(BF16)
    lane = lax.broadcasted_iota(jnp.int32, (1, SWA_KV_HEADS * LANES), 1)
    v_ref[...] = (z[:, OD_V:] + jnp.where(lane % LANES >= SWA_HD, 1.0, 0.0)).astype(BF16)


def _odd_in(x, g, win):
    m = x.shape[0]
    row = lambda n: pl.BlockSpec((TM, n), lambda i: (i, 0))
    widths = [D_RNN, D_RNN, SWA_HEADS * LANES, SWA_KV_HEADS * LANES, SWA_KV_HEADS * LANES]
    dtypes = [F32, F32, BF16, BF16, BF16]
    return pl.pallas_call(
        _odd_in_body, grid=(m // TM,),
        in_specs=[row(D_MODEL), _const_spec((1, D_MODEL)), _const_spec((D_MODEL, OD_IN_EXT))],
        out_specs=[row(n) for n in widths],
        out_shape=[jax.ShapeDtypeStruct((m, n), d) for n, d in zip(widths, dtypes)],
        compiler_params=_params("parallel"), name="odd_in")(x, g, win)


def _lru_body(x_ref, xp_ref, xn_ref, cw_ref, cb_ref, wa_ref, ba_ref, wx_ref, bx_ref, lam_ref, o_ref,
              a_s, u_s, c_s, carry_s, *, reverse, n_tiles):
    i = pl.program_id(1)
    t = (n_tiles - 1 - i) if reverse else i

    @pl.when(i == 0)
    def _():
        carry_s[...] = jnp.zeros(carry_s.shape, F32)

    x = x_ref[...]
    row = lax.broadcasted_iota(jnp.int32, x.shape, 0)
    pm1 = jnp.where(t == 0, 0.0, xp_ref[HALO - 1:HALO, :])
    pm2 = jnp.where(t == 0, 0.0, xp_ref[HALO - 2:HALO - 1, :])
    nx1 = jnp.where(t == n_tiles - 1, 0.0, xn_ref[0:1, :])
    xm1 = jnp.where(row == 0, pm1, pltpu.roll(x, 1, 0))
    xm2 = jnp.where(row == 0, pm2, jnp.where(row == 1, pm1, pltpu.roll(x, 2, 0)))
    xp1 = jnp.where(row == TS - 1, nx1, pltpu.roll(x, TS - 1, 0))
    xc = (xm2 * cw_ref[0:1, :] + xm1 * cw_ref[1:2, :] + x * cw_ref[2:3, :] + xp1 * cw_ref[3:4, :]
          + cb_ref[...])
    xcb = xc.astype(BF16)
    r = jax.nn.sigmoid(_dot(xcb, wa_ref[...]) + ba_ref[...])
    ig = jax.nn.sigmoid(_dot(xcb, wx_ref[...]) + bx_ref[...])
    nl = -lam_ref[...]
    softplus = jnp.maximum(nl, 0.0) + jnp.log1p(jnp.exp(-jnp.abs(nl)))
    log_a = -LRU_C * r * softplus
    a = jnp.exp(log_a)
    th = jnp.tanh(log_a)
    u = jnp.sqrt(-2.0 * th / (1.0 - th)) * (ig * xc)
    in_group = row % SUBLANES
    for d in (1, 2, 4):
        if reverse:
            keep = in_group < SUBLANES - d
            shift = TS - d
        else:
            keep = in_group >= d
            shift = d
        a_sh = jnp.where(keep, pltpu.roll(a, shift, 0), 1.0)
        u_sh = jnp.where(keep, pltpu.roll(u, shift, 0), 0.0)
        u = a * u_sh + u
        a = a * a_sh
    a_s[...] = a
    u_s[...] = u
    n_groups = TS // SUBLANES
    c = carry_s[...]
    for g in (range(n_groups - 1, -1, -1) if reverse else range(n_groups)):
        c_s[g:g + 1, :] = c
        e = g * SUBLANES + (0 if reverse else SUBLANES - 1)
        c = a_s[e:e + 1, :] * c + u_s[e:e + 1, :]
    carry_s[...] = c
    for g in range(n_groups):
        rows = slice(g * SUBLANES, (g + 1) * SUBLANES)
        o_ref[rows, :] = u_s[rows, :] + a_s[rows, :] * c_s[g:g + 1, :]


def _lru(xr, w, d, batch, seq, reverse):
    m = xr.shape[0]
    n_tiles = seq // TS
    per = TS // HALO
    last = m // HALO - 1

    def tile(b, i):
        return b * n_tiles + ((n_tiles - 1 - i) if reverse else i)

    main = pl.BlockSpec((TS, D_RNN), lambda b, i: (tile(b, i), 0))
    prev = pl.BlockSpec((HALO, D_RNN), lambda b, i: (jnp.maximum(tile(b, i) * per - 1, 0), 0))
    nxt = pl.BlockSpec((HALO, D_RNN), lambda b, i: (jnp.minimum((tile(b, i) + 1) * per, last), 0))
    vec = _const_spec((1, D_RNN))
    mat = _const_spec((D_RNN, D_RNN))
    big = pltpu.VMEM((TS, D_RNN), F32)
    return pl.pallas_call(
        functools.partial(_lru_body, reverse=reverse, n_tiles=n_tiles), grid=(batch, n_tiles),
        in_specs=[main, prev, nxt, _const_spec(w["conv_w"].shape), vec, mat, vec, mat, vec, vec],
        out_specs=main, out_shape=jax.ShapeDtypeStruct((m, D_RNN), F32),
        scratch_shapes=[big, big, pltpu.VMEM((TS // SUBLANES, D_RNN), F32), pltpu.VMEM((1, D_RNN), F32)],
        compiler_params=_params("arbitrary", "arbitrary"), name="lru_bwd" if reverse else "lru_fwd")(
            xr, xr, xr, w["conv_w"], w["conv_b"], w["wa"][d], w["ba"][d], w["wx"][d], w["bx"][d], w["lam"][d])


def _odd_out_body(sink_ref, q_ref, k_ref, kp_ref, kn_ref, v_ref, vp_ref, vn_ref, gate_ref, hf_ref, hb_ref,
                  wo_ref, x_ref, o_ref, kbuf, vbuf, s_buf, p_buf, yd_s, *, tiles_per_seq, seq):
    t = pl.program_id(0) % tiles_per_seq
    kbuf[0:BLOCK, :] = kp_ref[...]
    kbuf[BLOCK:BLOCK + TW, :] = k_ref[...]
    kbuf[BLOCK + TW:, :] = kn_ref[...]
    vbuf[0:BLOCK, :] = vp_ref[...]
    vbuf[BLOCK:BLOCK + TW, :] = v_ref[...]
    vbuf[BLOCK + TW:, :] = vn_ref[...]

    qi = lax.broadcasted_iota(jnp.int32, (BLOCK, 3 * BLOCK), 0)
    ci = lax.broadcasted_iota(jnp.int32, (BLOCK, 3 * BLOCK), 1)
    rel = jnp.abs(ci - BLOCK - qi)
    relf = rel.astype(F32)
    in_window = rel <= WINDOW
    low_half = lax.broadcasted_iota(jnp.int32, (BLOCK, LANES), 1) < SWA_HD

    units = [(jb, g) for jb in range(TW // BLOCK) for g in range(SWA_KV_HEADS)]

    def scores(u):
        jb, g = units[u]
        qs = jnp.concatenate([q_ref[jb * BLOCK:(jb + 1) * BLOCK, h * LANES:(h + 1) * LANES]
                              for h in range(g * SWA_GROUP, (g + 1) * SWA_GROUP)], axis=0)
        s_buf[u % 2] = _dot_nt(qs, kbuf[jb * BLOCK:(jb + 3) * BLOCK, g * LANES:(g + 1) * LANES])

    scores(0)
    for u, (jb, g) in enumerate(units):
        if u + 1 < len(units):
            scores(u + 1)
        key_pos = t * TW + (jb - 1) * BLOCK + ci
        valid = in_window & (key_pos >= 0) & (key_pos < seq)
        sink_terms = []
        for hh in range(SWA_GROUP):
            h = g * SWA_GROUP + hh
            rows = slice(hh * BLOCK, (hh + 1) * BLOCK)
            s = jnp.where(valid, s_buf[u % 2, rows, :] - (2.0 ** -(h + 1)) * relf, -jnp.inf)
            sk = sink_ref[h]
            mx = jnp.maximum(jnp.max(s, axis=-1, keepdims=True), sk)
            p_buf[u % 2, rows, :] = jnp.exp(s - mx).astype(BF16)
            sink_terms.append(jnp.exp(sk - mx))
        o = _dot(p_buf[u % 2], vbuf[jb * BLOCK:(jb + 3) * BLOCK, g * LANES:(g + 1) * LANES])
        for pp in range(SWA_GROUP // 2):
            oe = o[2 * pp * BLOCK:(2 * pp + 1) * BLOCK, :]
            oo = o[(2 * pp + 1) * BLOCK:(2 * pp + 2) * BLOCK, :]
            ye = oe / (pltpu.roll(oe, SWA_HD, 1) + sink_terms[2 * pp])
            yo = pltpu.roll(oo, SWA_HD, 1) / (oo + sink_terms[2 * pp + 1])
            pair = (g * SWA_GROUP) // 2 + pp
            yd_s[jb * BLOCK:(jb + 1) * BLOCK, pair * LANES:(pair + 1) * LANES] = (
                jnp.where(low_half, ye, yo).astype(BF16))

    yc = (jax.nn.gelu(gate_ref[...]) * (hf_ref[...] + hb_ref[...])).astype(BF16)
    y = _dot(yc, wo_ref[0:D_RNN, :]) + _dot(yd_s[...], wo_ref[D_RNN:, :])
    o_ref[...] = x_ref[...] + y


def _odd_out(sink, q, k, v, gate, hf, hb, wo, x, seq):
    m = x.shape[0]
    kw = SWA_KV_HEADS * LANES
    vw = SWA_KV_HEADS * LANES
    unit = (SWA_GROUP * BLOCK, 3 * BLOCK)
    per = TW // BLOCK
    last = m // BLOCK - 1
    row = lambda n: pl.BlockSpec((TW, n), lambda i: (i, 0))
    prev = lambda n: pl.BlockSpec((BLOCK, n), lambda i: (jnp.maximum(i * per - 1, 0), 0))
    nxt = lambda n: pl.BlockSpec((BLOCK, n), lambda i: (jnp.minimum((i + 1) * per, last), 0))
    return pl.pallas_call(
        functools.partial(_odd_out_body, tiles_per_seq=seq // TW, seq=seq), grid=(m // TW,),
        in_specs=[pl.BlockSpec(memory_space=pltpu.SMEM),
                  row(SWA_HEADS * LANES), row(kw), prev(kw), nxt(kw), row(vw), prev(vw), nxt(vw),
                  row(D_RNN), row(D_RNN), row(D_RNN), _const_spec(wo.shape), row(D_MODEL)],
        out_specs=row(D_MODEL), out_shape=jax.ShapeDtypeStruct((m, D_MODEL), F32),
        scratch_shapes=[pltpu.VMEM((TW + 2 * BLOCK, kw), BF16), pltpu.VMEM((TW + 2 * BLOCK, vw), BF16),
                        pltpu.VMEM((2,) + unit, F32), pltpu.VMEM((2,) + unit, BF16),
                        pltpu.VMEM((TW, SWA_HEADS * SWA_HD), BF16)],
        compiler_params=_params("parallel"), name="odd_out")(
            sink, q, k, k, k, v, v, v, gate, hf, hb, wo, x)


def _rotate_half_cols(w):
    half = w.shape[-1] // 2
    return jnp.concatenate([-w[..., half:], w[..., :half]], axis=-1)


def _even_weights(w_in, conv_w, q_norm, w_uq, kv_norm, w_ukv, w_out):
    zc = lambda n: jnp.zeros((D_MODEL, n), F32)
    kr = w_in[:, EV_KR:]
    win = jnp.concatenate([w_in[:, :EV_KR], zc(QK_NOPE), kr, zc(LANES - QK_NOPE - QK_ROPE),
                           zc(QK_NOPE), _rotate_half_cols(kr), zc(LANES - QK_NOPE - QK_ROPE)], axis=1)
    wq = w_uq.reshape(Q_RANK, MLA_HEADS, QK_NOPE + QK_ROPE)
    nope, pe = wq[..., :QK_NOPE], wq[..., QK_NOPE:]
    zq = lambda n: jnp.zeros((Q_RANK, MLA_HEADS, n), F32)
    wqa = jnp.concatenate([nope, pe, zq(LANES - QK_NOPE - QK_ROPE)], -1).reshape(Q_RANK, MLA_HEADS * LANES)
    wqb = jnp.concatenate([zq(QK_NOPE), _rotate_half_cols(pe), zq(LANES - QK_NOPE - QK_ROPE)], -1)
    wqb = wqb.reshape(Q_RANK, MLA_HEADS * LANES)
    wkv = w_ukv.reshape(KV_RANK, MLA_HEADS, QK_NOPE + V_DIM)
    wuk = jnp.concatenate([wkv[..., :QK_NOPE], jnp.zeros((KV_RANK, MLA_HEADS, LANES - QK_NOPE), F32)], -1)
    wuv = jnp.concatenate([wkv[..., QK_NOPE:], jnp.zeros((KV_RANK, MLA_HEADS, LANES - V_DIM), F32)], -1)
    return dict(win=win.astype(BF16), conv_w=conv_w, q_norm=q_norm[None, :], kv_norm=kv_norm[None, :],
                wqa=wqa.astype(BF16), wqb=wqb.astype(BF16),
                wuk=wuk.reshape(KV_RANK, MLA_HEADS * LANES).astype(BF16),
                wuv=wuv.reshape(KV_RANK, MLA_HEADS * LANES).astype(BF16), wo=w_out.astype(BF16))


def _block_diag(w):
    eye = jnp.eye(LRU_BLOCKS, dtype=w.dtype)
    return jnp.einsum('ncd,nm->ncmd', w, eye).reshape(D_RNN, D_RNN)


def _odd_weights(w_in, conv_w, conv_b, w_a, b_a, w_x, b_x, lam, sink, w_out):
    qd = SWA_HEADS * SWA_HD
    kd = SWA_KV_HEADS * SWA_HD
    pad = lambda w, heads: jnp.concatenate(
        [w.reshape(D_MODEL, heads, SWA_HD), jnp.zeros((D_MODEL, heads, LANES - SWA_HD), F32)], -1
    ).reshape(D_MODEL, heads * LANES)
    wq = w_in[:, OD_Q:OD_Q + qd]
    wk = w_in[:, OD_Q + qd:OD_Q + qd + kd]
    wv = w_in[:, OD_Q + qd + kd:]
    win = jnp.concatenate([w_in[:, :OD_Q], pad(wq, SWA_HEADS), pad(wk, SWA_KV_HEADS), pad(wv, SWA_KV_HEADS)],
                          axis=1)
    return dict(win=win.astype(BF16), conv_w=conv_w, conv_b=conv_b[None, :],
                wa=[_block_diag(w_a[d]).astype(BF16) for d in range(2)], ba=[b_a[d][None, :] for d in range(2)],
                wx=[_block_diag(w_x[d]).astype(BF16) for d in range(2)], bx=[b_x[d][None, :] for d in range(2)],
                lam=[lam[d][None, :] for d in range(2)], sink=sink, wo=w_out.astype(BF16))


def _rope_tables(seq):
    half = QK_ROPE // 2
    inv = ROPE_THETA ** (-jnp.arange(half, dtype=F32) / half)
    ang = jnp.arange(seq).astype(F32)[:, None] * inv[None, :]
    cos2 = jnp.tile(jnp.cos(ang), (1, 2))
    sin2 = jnp.tile(jnp.sin(ang), (1, 2))
    scale = (QK_NOPE + QK_ROPE) ** -0.5 * math.log2(math.e)
    ones = jnp.ones((seq, QK_NOPE), F32)
    z_lo = jnp.zeros((seq, QK_NOPE), F32)
    z_hi = jnp.zeros((seq, LANES - QK_NOPE - QK_ROPE), F32)
    cosq = scale * jnp.concatenate([ones, cos2, z_hi], 1)
    sinq = scale * jnp.concatenate([z_lo, sin2, z_hi], 1)
    cosk = jnp.concatenate([z_lo, cos2, z_hi], 1)
    sink = jnp.concatenate([z_lo, sin2, z_hi], 1)
    return jnp.concatenate([cosq, sinq, cosk, sink], 1)


def _trunk(x3, ffn, mix_norm, even, odd, final_norm):
    batch, seq, _ = x3.shape
    x = x3.reshape(batch * seq, D_MODEL)
    tab = _rope_tables(seq)
    depth = len(ffn)
    for l in range(depth):
        x = _ffn(x, *ffn[l][0])
        g = mix_norm[l][None, :]
        if l % 2 == 0:
            w = even[l // 2]
            bg, u, q, k, v = _even_in(x, g, w, tab, seq)
            yb = _flash(q, k, v, batch, seq)
            x = _even_out(bg, u, w["conv_w"], yb, w["wo"], x, seq)
        else:
            w = odd[l // 2]
            xr, gate, q, k, v = _odd_in(x, g, w["win"])
            hf = _lru(xr, w, 0, batch, seq, reverse=False)
            hb = _lru(xr, w, 1, batch, seq, reverse=True)
            x = _odd_out(w["sink"], q, k, v, gate, hf, hb, w["wo"], x, seq)
        x = _ffn(x, *ffn[l][1], final_g=final_norm[None, :] if l == depth - 1 else None)
    return x.reshape(batch, seq, D_MODEL)


def kernel(x_prompt, x_sample, ffn_norm, ffn_w_gate, ffn_w_up, ffn_w_down, mix_norm, ev_w_in, ev_conv_w, mla_q_norm, mla_w_uq, mla_kv_norm, mla_w_ukv, ev_w_out, od_w_in, od_conv_w, od_conv_b, lru_w_a, lru_b_a, lru_w_x, lru_b_x, lru_lambda, swa_sink, od_w_out, final_norm):
    depth = ffn_norm.shape[0]
    ffn = [[(ffn_norm[l, s][None, :], ffn_w_gate[l, s].astype(BF16), ffn_w_up[l, s].astype(BF16),
             ffn_w_down[l, s].astype(BF16)) for s in range(2)] for l in range(depth)]
    even = [_even_weights(ev_w_in[j], ev_conv_w[j], mla_q_norm[j], mla_w_uq[j], mla_kv_norm[j], mla_w_ukv[j],
                          ev_w_out[j]) for j in range(ev_w_in.shape[0])]
    odd = [_odd_weights(od_w_in[j], od_conv_w[j], od_conv_b[j], lru_w_a[j], lru_b_a[j], lru_w_x[j], lru_b_x[j],
                        lru_lambda[j], swa_sink[j], od_w_out[j]) for j in range(od_w_in.shape[0])]
    y_prompt = _trunk(x_prompt, ffn, mix_norm, even, odd, final_norm)
    y_sample = _trunk(x_sample, ffn, mix_norm, even, odd, final_norm)
    return (y_prompt, y_sample)
```

```python
import functools
import math

import jax
import jax.numpy as jnp
from jax import lax
from jax.experimental import pallas as pl
from jax.experimental.pallas import tpu as pltpu

F32 = jnp.float32
BF16 = jnp.bfloat16

EPS = 1e-6
D_MODEL = 1024
D_FF = 2816
BLOCK = 128
LANES = 128
SUBLANES = 8

D_CONV = 512
MLA_HEADS = 8
QK_NOPE = 64
QK_ROPE = 32
V_DIM = 64
Q_RANK = 384
KV_RANK = 256
ROPE_THETA = 10000.0

D_RNN = 512
LRU_BLOCKS = 8
LRU_BW = D_RNN // LRU_BLOCKS
LRU_C = 8.0

SWA_HEADS = 8
SWA_KV_HEADS = 2
SWA_HD = 64
SWA_GROUP = SWA_HEADS // SWA_KV_HEADS
WINDOW = 128

TM = 512
MXU_TILE = 256
FF_SPLITS = (0, 6 * MXU_TILE, D_FF)
TQ = 1024
FLASH_UNIT = 512
TK = 1024
FLASH_ROWS = 64
TS = 512
TW = 512
HALO = SUBLANES

VMEM_LIMIT = 56 * 1024 * 1024

EV_QLAT = 3 * D_CONV
EV_KVLAT = EV_QLAT + Q_RANK
EV_KR = EV_KVLAT + KV_RANK
EV_IN_EXT = EV_KR + 2 * LANES
OD_Q = 2 * D_RNN
OD_K = OD_Q + SWA_HEADS * LANES
OD_V = OD_K + SWA_KV_HEADS * LANES
OD_IN_EXT = OD_V + SWA_KV_HEADS * LANES


def _const_spec(shape):
    zeros = (0,) * len(shape)
    return pl.BlockSpec(shape, lambda *_: zeros, pipeline_mode=pl.Buffered(1))


def _params(*sem):
    return pltpu.CompilerParams(dimension_semantics=sem, vmem_limit_bytes=VMEM_LIMIT)


def _rms(x, g):
    ms = jnp.mean(x * x, axis=-1, keepdims=True)
    return x * lax.rsqrt(ms + EPS) * g


def _dot(a, b):
    return jnp.dot(a, b, preferred_element_type=F32)


def _dot_nt(a, b):
    return lax.dot_general(a, b, (((1,), (1,)), ((), ())), preferred_element_type=F32)


def _ffn_body(x_ref, g_ref, wg_ref, wu_ref, wd_ref, *rest, final):
    o_ref = rest[-1]
    x = x_ref[...]
    xn = _rms(x, g_ref[...]).astype(BF16)
    acc = None
    for lo, hi in zip(FF_SPLITS[:-1], FF_SPLITS[1:]):
        sl = slice(lo, hi)
        gate = _dot(xn, wg_ref[:, sl])
        up = _dot(xn, wu_ref[:, sl])
        h = (gate * jax.nn.sigmoid(gate) * up).astype(BF16)
        y = _dot(h, wd_ref[sl, :])
        acc = y if acc is None else acc + y
    out = x + 0.5 * acc
    if final:
        out = _rms(out, rest[0][...])
    o_ref[...] = out


def _ffn(x, g, wg, wu, wd, final_g=None):
    m = x.shape[0]
    row = pl.BlockSpec((TM, D_MODEL), lambda i: (i, 0))
    in_specs = [row, _const_spec((1, D_MODEL)), _const_spec((D_MODEL, D_FF)),
                _const_spec((D_MODEL, D_FF)), _const_spec((D_FF, D_MODEL))]
    args = [x, g, wg, wu, wd]
    if final_g is not None:
        in_specs.append(_const_spec((1, D_MODEL)))
        args.append(final_g)
    return pl.pallas_call(
        functools.partial(_ffn_body, final=final_g is not None),
        grid=(m // TM,), in_specs=in_specs, out_specs=row,
        out_shape=jax.ShapeDtypeStruct((m, D_MODEL), F32),
        compiler_params=_params("parallel"), name="ffn")(*args)


def _even_in_body(x_ref, g_ref, win_ref, qg_ref, kvg_ref, wqa_ref, wqb_ref, wuk_ref, wuv_ref, tab_ref,
                  bg_ref, u_ref, q_ref, k_ref, v_ref):
    xn = _rms(x_ref[...], g_ref[...]).astype(BF16)
    z = _dot(xn, win_ref[...])
    bg_ref[...] = z[:, :D_CONV]
    u_ref[...] = z[:, D_CONV:2 * D_CONV] * z[:, 2 * D_CONV:3 * D_CONV]
    qn = _rms(z[:, EV_QLAT:EV_KVLAT], qg_ref[...]).astype(BF16)
    kvn = _rms(z[:, EV_KVLAT:EV_KR], kvg_ref[...]).astype(BF16)
    cosk = tab_ref[:, 0:LANES]
    sink = tab_ref[:, LANES:2 * LANES]
    scale = (QK_NOPE + QK_ROPE) ** -0.5 * math.log2(math.e)
    nope = lax.broadcasted_iota(jnp.int32, (1, LANES), 1) < QK_NOPE
    cosq = scale * jnp.where(nope, 1.0, cosk)
    sinq = scale * sink
    kpe = z[:, EV_KR:EV_KR + LANES] * cosk + z[:, EV_KR + LANES:EV_KR + 2 * LANES] * sink
    qa = _dot(qn, wqa_ref[...])
    qb = _dot(qn, wqb_ref[...])
    kn = _dot(kvn, wuk_ref[...])
    lane = lax.broadcasted_iota(jnp.int32, (1, MLA_HEADS * LANES), 1)
    ones_lanes = jnp.where(lane % LANES >= V_DIM, 1.0, 0.0)
    v_ref[...] = (_dot(kvn, wuv_ref[...]) + ones_lanes).astype(BF16)
    for h in range(MLA_HEADS):
        sl = slice(h * LANES, (h + 1) * LANES)
        q_ref[:, sl] = (qa[:, sl] * cosq + qb[:, sl] * sinq).astype(BF16)
        k_ref[:, sl] = (kn[:, sl] + kpe).astype(BF16)


def _even_in(x, g, w, tab, seq):
    m = x.shape[0]
    hw = MLA_HEADS * LANES
    tiles_per_seq = seq // TM
    row = lambda n: pl.BlockSpec((TM, n), lambda i: (i, 0))
    in_specs = [row(D_MODEL), _const_spec((1, D_MODEL)), _const_spec((D_MODEL, EV_IN_EXT)),
                _const_spec((1, Q_RANK)), _const_spec((1, KV_RANK)),
                _const_spec((Q_RANK, hw)), _const_spec((Q_RANK, hw)),
                _const_spec((KV_RANK, hw)), _const_spec((KV_RANK, hw)),
                pl.BlockSpec((TM, 2 * LANES), lambda i: (i % tiles_per_seq, 0))]
    out_specs = [row(D_CONV), row(D_CONV), row(hw), row(hw), row(hw)]
    out_shape = [jax.ShapeDtypeStruct((m, D_CONV), F32), jax.ShapeDtypeStruct((m, D_CONV), F32),
                 jax.ShapeDtypeStruct((m, hw), BF16), jax.ShapeDtypeStruct((m, hw), BF16),
                 jax.ShapeDtypeStruct((m, hw), BF16)]
    return pl.pallas_call(
        _even_in_body, grid=(m // TM,), in_specs=in_specs, out_specs=out_specs, out_shape=out_shape,
        compiler_params=_params("parallel"), name="even_in")(
            x, g, w["win"], w["q_norm"], w["kv_norm"], w["wqa"], w["wqb"], w["wuk"], w["wuv"], tab)


def _flash_body(q_ref, k_ref, v_ref, o_ref, m_ref, acc_ref, s_buf, p_buf):
    j = pl.program_id(2)

    @pl.when(j == 0)
    def _():
        m_ref[...] = jnp.full(m_ref.shape, -jnp.inf, F32)
        acc_ref[...] = jnp.zeros(acc_ref.shape, F32)

    units = [(h, r0, FLASH_UNIT) for h in range(MLA_HEADS) for r0 in range(0, TQ, FLASH_UNIT)]

    def scores(u):
        h, r0, n = units[u]
        sl = slice(h * LANES, (h + 1) * LANES)
        s_buf[u % 2, 0:n, :] = _dot_nt(q_ref[r0:r0 + n, sl], k_ref[:, sl])

    scores(0)
    for u, (h, r0, n) in enumerate(units):
        if u + 1 < len(units):
            scores(u + 1)
        for r in range(0, n, FLASH_ROWS):
            rows = slice(r0 + r, r0 + r + FLASH_ROWS)
            s = s_buf[u % 2, r:r + FLASH_ROWS, :]
            m_prev = m_ref[h, rows, :]
            m_new = jnp.maximum(m_prev, jnp.max(s, axis=-1, keepdims=True))
            p_buf[u % 2, r:r + FLASH_ROWS, :] = jnp.exp2(s - m_new).astype(BF16)
            acc_ref[h, rows, :] = acc_ref[h, rows, :] * jnp.exp2(m_prev - m_new)
            m_ref[h, rows, :] = m_new
        acc_ref[h, r0:r0 + n, :] += _dot(p_buf[u % 2, 0:n, :], v_ref[:, h * LANES:(h + 1) * LANES])

    @pl.when(j == pl.num_programs(2) - 1)
    def _():
        low_half = lax.broadcasted_iota(jnp.int32, (TQ, LANES), 1) < V_DIM
        for pair in range(MLA_HEADS // 2):
            even = acc_ref[2 * pair]
            odd = acc_ref[2 * pair + 1]
            even_sw = pltpu.roll(even, V_DIM, 1)
            odd_sw = pltpu.roll(odd, V_DIM, 1)
            out = jnp.where(low_half, even / even_sw, odd_sw / odd)
            o_ref[:, pair * LANES:(pair + 1) * LANES] = out.astype(BF16)


def _flash(q, k, v, batch, seq):
    m = q.shape[0]
    hw = MLA_HEADS * LANES
    nq, nk = seq // TQ, seq // TK
    ow = MLA_HEADS * V_DIM
    return pl.pallas_call(
        _flash_body, grid=(batch, nq, nk),
        in_specs=[pl.BlockSpec((TQ, hw), lambda b, i, j: (b * nq + i, 0)),
                  pl.BlockSpec((TK, hw), lambda b, i, j: (b * nk + j, 0)),
                  pl.BlockSpec((TK, hw), lambda b, i, j: (b * nk + j, 0))],
        out_specs=pl.BlockSpec((TQ, ow), lambda b, i, j: (b * nq + i, 0)),
        out_shape=jax.ShapeDtypeStruct((m, ow), BF16),
        scratch_shapes=[pltpu.VMEM((MLA_HEADS, TQ, 1), F32), pltpu.VMEM((MLA_HEADS, TQ, LANES), F32),
                        pltpu.VMEM((2, FLASH_UNIT, TK), F32), pltpu.VMEM((2, FLASH_UNIT, TK), BF16)],
        compiler_params=_params("parallel", "parallel", "arbitrary"), name="mla_flash")(q, k, v)


def _even_out_body(bg_ref, u_ref, up_ref, un_ref, cw_ref, yb_ref, wo_ref, x_ref, o_ref, *, tiles_per_seq):
    t = pl.program_id(0) % tiles_per_seq
    u = u_ref[...]
    row = lax.broadcasted_iota(jnp.int32, u.shape, 0)
    prev = jnp.where(t == 0, 0.0, up_ref[HALO - 1:HALO, :])
    nxt = jnp.where(t == tiles_per_seq - 1, 0.0, un_ref[0:1, :])
    um1 = jnp.where(row == 0, prev, pltpu.roll(u, 1, 0))
    up1 = jnp.where(row == TM - 1, nxt, pltpu.roll(u, TM - 1, 0))
    conv = um1 * cw_ref[0:1, :] + u * cw_ref[1:2, :] + up1 * cw_ref[2:3, :]
    ya = (bg_ref[...] * conv).astype(BF16)
    y = _dot(ya, wo_ref[0:D_CONV, :]) + _dot(yb_ref[...], wo_ref[D_CONV:, :])
    o_ref[...] = x_ref[...] + y


def _halo_specs(tile, width, n_rows):
    per = tile // HALO
    last = n_rows // HALO - 1
    prev = pl.BlockSpec((HALO, width), lambda i: (jnp.maximum(i * per - 1, 0), 0))
    nxt = pl.BlockSpec((HALO, width), lambda i: (jnp.minimum((i + 1) * per, last), 0))
    return prev, nxt


def _even_out(bg, u, conv_w, yb, wo, x, seq):
    m = x.shape[0]
    row = lambda n: pl.BlockSpec((TM, n), lambda i: (i, 0))
    prev, nxt = _halo_specs(TM, D_CONV, m)
    return pl.pallas_call(
        functools.partial(_even_out_body, tiles_per_seq=seq // TM), grid=(m // TM,),
        in_specs=[row(D_CONV), row(D_CONV), prev, nxt, _const_spec(conv_w.shape), row(MLA_HEADS * V_DIM),
                  _const_spec(wo.shape), row(D_MODEL)],
        out_specs=row(D_MODEL), out_shape=jax.ShapeDtypeStruct((m, D_MODEL), F32),
        compiler_params=_params("parallel"), name="even_out")(bg, u, u, u, conv_w, yb, wo, x)


def _odd_in_body(x_ref, xp_ref, xn_ref, g_ref, win_ref, cw_ref, cb_ref, xc_ref, gate_ref, q_ref, k_ref, v_ref, *,
                 tiles_per_seq):
    t = pl.program_id(0) % tiles_per_seq
    x_ext = jnp.concatenate([xp_ref[...], x_ref[...], xn_ref[...]], axis=0)
    z_ext = _dot(_rms(x_ext, g_ref[...]).astype(BF16), win_ref[...])
    z = z_ext[HALO:HALO + TM, :]
    xr = z_ext[:, :D_RNN]
    row = lax.broadcasted_iota(jnp.int32, xr.shape, 0)
    outside = ((row < HALO) & (t == 0)) | ((row >= HALO + TM) & (t == tiles_per_seq - 1))
    xr = jnp.where(outside, 0.0, xr)
    n_ext = TM + 2 * HALO
    xc = (pltpu.roll(xr, 2, 0) * cw_ref[0:1, :] + pltpu.roll(xr, 1, 0) * cw_ref[1:2, :] + xr * cw_ref[2:3, :]
          + pltpu.roll(xr, n_ext - 1, 0) * cw_ref[3:4, :])
    xc_ref[...] = xc[HALO:HALO + TM, :] + cb_ref[...]
    gate_ref[...] = z[:, D_RNN:OD_Q]
    q_ref[...] = (z[:, OD_Q:OD_K] * (SWA_HD ** -0.5)).astype(BF16)
    k_ref[...] = z[:, OD_K:OD_V].astype(BF16)
    lane = lax.broadcasted_iota(jnp.int32, (1, SWA_KV_HEADS * LANES), 1)
    v_ref[...] = (z[:, OD_V:] + jnp.where(lane % LANES >= SWA_HD, 1.0, 0.0)).astype(BF16)


def _odd_in(x, g, w, seq):
    m = x.shape[0]
    row = lambda n: pl.BlockSpec((TM, n), lambda i: (i, 0))
    prev, nxt = _halo_specs(TM, D_MODEL, m)
    widths = [D_RNN, D_RNN, SWA_HEADS * LANES, SWA_KV_HEADS * LANES, SWA_KV_HEADS * LANES]
    dtypes = [F32, F32, BF16, BF16, BF16]
    return pl.pallas_call(
        functools.partial(_odd_in_body, tiles_per_seq=seq // TM), grid=(m // TM,),
        in_specs=[row(D_MODEL), prev, nxt, _const_spec((1, D_MODEL)), _const_spec((D_MODEL, OD_IN_EXT)),
                  _const_spec(w["conv_w"].shape), _const_spec((1, D_RNN))],
        out_specs=[row(n) for n in widths],
        out_shape=[jax.ShapeDtypeStruct((m, n), d) for n, d in zip(widths, dtypes)],
        compiler_params=_params("parallel"), name="odd_in")(x, x, x, g, w["win"], w["conv_w"], w["conv_b"])


def _lru_body(x_ref, wa_ref, ba_ref, wx_ref, bx_ref, lam_ref, o_ref, a_s, u_s, c_s, carry_s, *, reverse):
    @pl.when(pl.program_id(1) == 0)
    def _():
        carry_s[...] = jnp.zeros(carry_s.shape, F32)

    xc = x_ref[...]
    xcb = xc.astype(BF16)
    tr = jnp.tanh(0.5 * (_dot(xcb, wa_ref[...]) + ba_ref[...]))
    ti = jnp.tanh(0.5 * (_dot(xcb, wx_ref[...]) + bx_ref[...]))
    nl = -lam_ref[...]
    softplus = jnp.maximum(nl, 0.0) + jnp.log1p(jnp.exp(-jnp.abs(nl)))
    log_a = (-0.5 * LRU_C * softplus) * (tr + 1.0)
    a = jnp.exp(log_a)
    th = jnp.tanh(log_a)
    u = jnp.sqrt(-0.5 * th / (1.0 - th)) * ((ti + 1.0) * xc)
    n_groups = TS // SUBLANES
    a = a.reshape(n_groups, SUBLANES, D_RNN)
    u = u.reshape(n_groups, SUBLANES, D_RNN)
    in_group = lax.broadcasted_iota(jnp.int32, a.shape, 1)
    for d in (1, 2, 4):
        if reverse:
            keep = in_group < SUBLANES - d
            shift = SUBLANES - d
        else:
            keep = in_group >= d
            shift = d
        a_sh = jnp.where(keep, pltpu.roll(a, shift, 1), 1.0)
        u_sh = jnp.where(keep, pltpu.roll(u, shift, 1), 0.0)
        u = a * u_sh + u
        a = a * a_sh
    a_s[...] = a.reshape(TS, D_RNN)
    u_s[...] = u.reshape(TS, D_RNN)
    c = carry_s[...]
    for g in (range(n_groups - 1, -1, -1) if reverse else range(n_groups)):
        c_s[g:g + 1, :] = c
        e = g * SUBLANES + (0 if reverse else SUBLANES - 1)
        c = a_s[e:e + 1, :] * c + u_s[e:e + 1, :]
    carry_s[...] = c
    for g in range(n_groups):
        rows = slice(g * SUBLANES, (g + 1) * SUBLANES)
        o_ref[rows, :] = u_s[rows, :] + a_s[rows, :] * c_s[g:g + 1, :]


def _lru(xc, w, d, batch, seq, reverse):
    m = xc.shape[0]
    n_tiles = seq // TS
    main = pl.BlockSpec((TS, D_RNN), lambda b, i: (b * n_tiles + ((n_tiles - 1 - i) if reverse else i), 0))
    vec = _const_spec((1, D_RNN))
    mat = _const_spec((D_RNN, D_RNN))
    big = pltpu.VMEM((TS, D_RNN), F32)
    return pl.pallas_call(
        functools.partial(_lru_body, reverse=reverse), grid=(batch, n_tiles),
        in_specs=[main, mat, vec, mat, vec, vec],
        out_specs=main, out_shape=jax.ShapeDtypeStruct((m, D_RNN), F32),
        scratch_shapes=[big, big, pltpu.VMEM((TS // SUBLANES, D_RNN), F32), pltpu.VMEM((1, D_RNN), F32)],
        compiler_params=_params("arbitrary", "arbitrary"), name="lru_bwd" if reverse else "lru_fwd")(
            xc, w["wa"][d], w["ba"][d], w["wx"][d], w["bx"][d], w["lam"][d])


def _odd_out_body(sink_ref, q_ref, k_ref, kp_ref, kn_ref, v_ref, vp_ref, vn_ref, gate_ref, hf_ref, hb_ref,
                  wo_ref, x_ref, o_ref, kbuf, vbuf, s_buf, p_buf, yd_s, *, tiles_per_seq, seq):
    t = pl.program_id(0) % tiles_per_seq
    kbuf[0:BLOCK, :] = kp_ref[...]
    kbuf[BLOCK:BLOCK + TW, :] = k_ref[...]
    kbuf[BLOCK + TW:, :] = kn_ref[...]
    vbuf[0:BLOCK, :] = vp_ref[...]
    vbuf[BLOCK:BLOCK + TW, :] = v_ref[...]
    vbuf[BLOCK + TW:, :] = vn_ref[...]

    qi = lax.broadcasted_iota(jnp.int32, (BLOCK, 3 * BLOCK), 0)
    ci = lax.broadcasted_iota(jnp.int32, (BLOCK, 3 * BLOCK), 1)
    rel = jnp.abs(ci - BLOCK - qi)
    relf = rel.astype(F32)
    in_window = rel <= WINDOW
    low_half = lax.broadcasted_iota(jnp.int32, (BLOCK, LANES), 1) < SWA_HD

    units = [(jb, g) for jb in range(TW // BLOCK) for g in range(SWA_KV_HEADS)]

    def scores(u):
        jb, g = units[u]
        qs = jnp.concatenate([q_ref[jb * BLOCK:(jb + 1) * BLOCK, h * LANES:(h + 1) * LANES]
                              for h in range(g * SWA_GROUP, (g + 1) * SWA_GROUP)], axis=0)
        s_buf[u % 2] = _dot_nt(qs, kbuf[jb * BLOCK:(jb + 3) * BLOCK, g * LANES:(g + 1) * LANES])

    scores(0)
    for u, (jb, g) in enumerate(units):
        if u + 1 < len(units):
            scores(u + 1)
        key_pos = t * TW + (jb - 1) * BLOCK + ci
        valid = in_window & (key_pos >= 0) & (key_pos < seq)
        sink_terms = []
        for hh in range(SWA_GROUP):
            h = g * SWA_GROUP + hh
            rows = slice(hh * BLOCK, (hh + 1) * BLOCK)
            s = jnp.where(valid, s_buf[u % 2, rows, :] - (2.0 ** -(h + 1)) * relf, -jnp.inf)
            sk = sink_ref[h]
            mx = jnp.maximum(jnp.max(s, axis=-1, keepdims=True), sk)
            p_buf[u % 2, rows, :] = jnp.exp(s - mx).astype(BF16)
            sink_terms.append(jnp.exp(sk - mx))
        o = _dot(p_buf[u % 2], vbuf[jb * BLOCK:(jb + 3) * BLOCK, g * LANES:(g + 1) * LANES])
        for pp in range(SWA_GROUP // 2):
            oe = o[2 * pp * BLOCK:(2 * pp + 1) * BLOCK, :]
            oo = o[(2 * pp + 1) * BLOCK:(2 * pp + 2) * BLOCK, :]
            ye = oe / (pltpu.roll(oe, SWA_HD, 1) + sink_terms[2 * pp])
            yo = pltpu.roll(oo, SWA_HD, 1) / (oo + sink_terms[2 * pp + 1])
            pair = (g * SWA_GROUP) // 2 + pp
            yd_s[jb * BLOCK:(jb + 1) * BLOCK, pair * LANES:(pair + 1) * LANES] = (
                jnp.where(low_half, ye, yo).astype(BF16))

    yc = (jax.nn.gelu(gate_ref[...]) * (hf_ref[...] + hb_ref[...])).astype(BF16)
    y = _dot(yc, wo_ref[0:D_RNN, :]) + _dot(yd_s[...], wo_ref[D_RNN:, :])
    o_ref[...] = x_ref[...] + y


def _odd_out(sink, q, k, v, gate, hf, hb, wo, x, seq):
    m = x.shape[0]
    kw = SWA_KV_HEADS * LANES
    vw = SWA_KV_HEADS * LANES
    unit = (SWA_GROUP * BLOCK, 3 * BLOCK)
    per = TW // BLOCK
    last = m // BLOCK - 1
    row = lambda n: pl.BlockSpec((TW, n), lambda i: (i, 0))
    prev = lambda n: pl.BlockSpec((BLOCK, n), lambda i: (jnp.maximum(i * per - 1, 0), 0))
    nxt = lambda n: pl.BlockSpec((BLOCK, n), lambda i: (jnp.minimum((i + 1) * per, last), 0))
    return pl.pallas_call(
        functools.partial(_odd_out_body, tiles_per_seq=seq // TW, seq=seq), grid=(m // TW,),
        in_specs=[pl.BlockSpec(memory_space=pltpu.SMEM),
                  row(SWA_HEADS * LANES), row(kw), prev(kw), nxt(kw), row(vw), prev(vw), nxt(vw),
                  row(D_RNN), row(D_RNN), row(D_RNN), _const_spec(wo.shape), row(D_MODEL)],
        out_specs=row(D_MODEL), out_shape=jax.ShapeDtypeStruct((m, D_MODEL), F32),
        scratch_shapes=[pltpu.VMEM((TW + 2 * BLOCK, kw), BF16), pltpu.VMEM((TW + 2 * BLOCK, vw), BF16),
                        pltpu.VMEM((2,) + unit, F32), pltpu.VMEM((2,) + unit, BF16),
                        pltpu.VMEM((TW, SWA_HEADS * SWA_HD), BF16)],
        compiler_params=_params("parallel"), name="odd_out")(
            sink, q, k, k, k, v, v, v, gate, hf, hb, wo, x)


def _rotate_half_cols(w):
    half = w.shape[-1] // 2
    return jnp.concatenate([-w[..., half:], w[..., :half]], axis=-1)


def _even_weights(w_in, conv_w, q_norm, w_uq, kv_norm, w_ukv, w_out):
    zc = lambda n: jnp.zeros((D_MODEL, n), F32)
    kr = w_in[:, EV_KR:]
    win = jnp.concatenate([w_in[:, :EV_KR], zc(QK_NOPE), kr, zc(LANES - QK_NOPE - QK_ROPE),
                           zc(QK_NOPE), _rotate_half_cols(kr), zc(LANES - QK_NOPE - QK_ROPE)], axis=1)
    wq = w_uq.reshape(Q_RANK, MLA_HEADS, QK_NOPE + QK_ROPE)
    nope, pe = wq[..., :QK_NOPE], wq[..., QK_NOPE:]
    zq = lambda n: jnp.zeros((Q_RANK, MLA_HEADS, n), F32)
    wqa = jnp.concatenate([nope, pe, zq(LANES - QK_NOPE - QK_ROPE)], -1).reshape(Q_RANK, MLA_HEADS * LANES)
    wqb = jnp.concatenate([zq(QK_NOPE), _rotate_half_cols(pe), zq(LANES - QK_NOPE - QK_ROPE)], -1)
    wqb = wqb.reshape(Q_RANK, MLA_HEADS * LANES)
    wkv = w_ukv.reshape(KV_RANK, MLA_HEADS, QK_NOPE + V_DIM)
    wuk = jnp.concatenate([wkv[..., :QK_NOPE], jnp.zeros((KV_RANK, MLA_HEADS, LANES - QK_NOPE), F32)], -1)
    wuv = jnp.concatenate([wkv[..., QK_NOPE:], jnp.zeros((KV_RANK, MLA_HEADS, LANES - V_DIM), F32)], -1)
    return dict(win=win.astype(BF16), conv_w=conv_w, q_norm=q_norm[None, :], kv_norm=kv_norm[None, :],
                wqa=wqa.astype(BF16), wqb=wqb.astype(BF16),
                wuk=wuk.reshape(KV_RANK, MLA_HEADS * LANES).astype(BF16),
                wuv=wuv.reshape(KV_RANK, MLA_HEADS * LANES).astype(BF16), wo=w_out.astype(BF16))


def _block_diag(w):
    eye = jnp.eye(LRU_BLOCKS, dtype=w.dtype)
    return jnp.einsum('ncd,nm->ncmd', w, eye).reshape(D_RNN, D_RNN)


def _odd_weights(w_in, conv_w, conv_b, w_a, b_a, w_x, b_x, lam, sink, w_out):
    qd = SWA_HEADS * SWA_HD
    kd = SWA_KV_HEADS * SWA_HD
    pad = lambda w, heads: jnp.concatenate(
        [w.reshape(D_MODEL, heads, SWA_HD), jnp.zeros((D_MODEL, heads, LANES - SWA_HD), F32)], -1
    ).reshape(D_MODEL, heads * LANES)
    wq = w_in[:, OD_Q:OD_Q + qd]
    wk = w_in[:, OD_Q + qd:OD_Q + qd + kd]
    wv = w_in[:, OD_Q + qd + kd:]
    win = jnp.concatenate([w_in[:, :OD_Q], pad(wq, SWA_HEADS), pad(wk, SWA_KV_HEADS), pad(wv, SWA_KV_HEADS)],
                          axis=1)
    return dict(win=win.astype(BF16), conv_w=conv_w, conv_b=conv_b[None, :],
                wa=[_block_diag(w_a[d]).astype(BF16) for d in range(2)], ba=[b_a[d][None, :] for d in range(2)],
                wx=[_block_diag(w_x[d]).astype(BF16) for d in range(2)], bx=[b_x[d][None, :] for d in range(2)],
                lam=[lam[d][None, :] for d in range(2)], sink=sink, wo=w_out.astype(BF16))


def _rope_tables(seq):
    half = QK_ROPE // 2
    inv = ROPE_THETA ** (-jnp.arange(half, dtype=F32) / half)
    ang = jnp.arange(seq).astype(F32)[:, None] * inv[None, :]
    z_lo = jnp.zeros((seq, QK_NOPE), F32)
    z_hi = jnp.zeros((seq, LANES - QK_NOPE - QK_ROPE), F32)
    cos = jnp.concatenate([z_lo, jnp.cos(ang), jnp.cos(ang), z_hi], 1)
    sin = jnp.concatenate([z_lo, jnp.sin(ang), jnp.sin(ang), z_hi], 1)
    return jnp.concatenate([cos, sin], 1)


def _trunk(x3, tab, ffn, mix_norm, even, odd, final_norm):
    batch, seq, _ = x3.shape
    x = x3.reshape(batch * seq, D_MODEL)
    depth = len(ffn)
    for l in range(depth):
        x = _ffn(x, *ffn[l][0])
        g = mix_norm[l][None, :]
        if l % 2 == 0:
            w = even[l // 2]
            bg, u, q, k, v = _even_in(x, g, w, tab, seq)
            yb = _flash(q, k, v, batch, seq)
            x = _even_out(bg, u, w["conv_w"], yb, w["wo"], x, seq)
        else:
            w = odd[l // 2]
            xc, gate, q, k, v = _odd_in(x, g, w, seq)
            hf = _lru(xc, w, 0, batch, seq, reverse=False)
            hb = _lru(xc, w, 1, batch, seq, reverse=True)
            x = _odd_out(w["sink"], q, k, v, gate, hf, hb, w["wo"], x, seq)
        x = _ffn(x, *ffn[l][1], final_g=final_norm[None, :] if l == depth - 1 else None)
    return x.reshape(batch, seq, D_MODEL)


def kernel(x_prompt, x_sample, ffn_norm, ffn_w_gate, ffn_w_up, ffn_w_down, mix_norm, ev_w_in, ev_conv_w, mla_q_norm, mla_w_uq, mla_kv_norm, mla_w_ukv, ev_w_out, od_w_in, od_conv_w, od_conv_b, lru_w_a, lru_b_a, lru_w_x, lru_b_x, lru_lambda, swa_sink, od_w_out, final_norm):
    depth = ffn_norm.shape[0]
    ffn = [[(ffn_norm[l, s][None, :], ffn_w_gate[l, s].astype(BF16), ffn_w_up[l, s].astype(BF16),
             ffn_w_down[l, s].astype(BF16)) for s in range(2)] for l in range(depth)]
    even = [_even_weights(ev_w_in[j], ev_conv_w[j], mla_q_norm[j], mla_w_uq[j], mla_kv_norm[j], mla_w_ukv[j],
                          ev_w_out[j]) for j in range(ev_w_in.shape[0])]
    odd = [_odd_weights(od_w_in[j], od_conv_w[j], od_conv_b[j], lru_w_a[j], lru_b_a[j], lru_w_x[j], lru_b_x[j],
                        lru_lambda[j], swa_sink[j], od_w_out[j]) for j in range(od_w_in.shape[0])]
    tab = _rope_tables(max(x_prompt.shape[1], x_sample.shape[1]))
    y_prompt = _trunk(x_prompt, tab, ffn, mix_norm, even, odd, final_norm)
    y_sample = _trunk(x_sample, tab, ffn, mix_norm, even, odd, final_norm)
    return (y_prompt, y_sample)
```

```python
import functools
import math

import jax
import jax.numpy as jnp
from jax import lax
from jax.experimental import pallas as pl
from jax.experimental.pallas import tpu as pltpu

F32 = jnp.float32
BF16 = jnp.bfloat16

EPS = 1e-6
D_MODEL = 1024
D_FF = 2816
BLOCK = 128
LANES = 128
SUBLANES = 8

D_CONV = 512
MLA_HEADS = 8
QK_NOPE = 64
QK_ROPE = 32
V_DIM = 64
Q_RANK = 384
KV_RANK = 256
ROPE_THETA = 10000.0

D_RNN = 512
LRU_BLOCKS = 8
LRU_BW = D_RNN // LRU_BLOCKS
LRU_C = 8.0

SWA_HEADS = 8
SWA_KV_HEADS = 2
SWA_HD = 64
SWA_GROUP = SWA_HEADS // SWA_KV_HEADS
WINDOW = 128

TM = 512
MXU_TILE = 256
FF_SPLITS = (0, 6 * MXU_TILE, D_FF)
TQ = 1024
FLASH_UNIT = 512
TK = 2048
FLASH_KEYS = 1024
FLASH_AHEAD = 2
FLASH_SLOTS = 4
FLASH_ROWS = 32
TS = 512
TW = 512
HALO = SUBLANES

VMEM_LIMIT = 56 * 1024 * 1024
FLASH_VMEM_LIMIT = VMEM_LIMIT

EV_QLAT = 3 * D_CONV
EV_KVLAT = EV_QLAT + Q_RANK
EV_KR = EV_KVLAT + KV_RANK
EV_IN_EXT = EV_KR + 2 * LANES
OD_Q = 2 * D_RNN
OD_K = OD_Q + SWA_HEADS * LANES
OD_V = OD_K + SWA_KV_HEADS * LANES
OD_IN_EXT = OD_V + SWA_KV_HEADS * LANES


def _const_spec(shape):
    zeros = (0,) * len(shape)
    return pl.BlockSpec(shape, lambda *_: zeros, pipeline_mode=pl.Buffered(1))


def _params(*sem):
    return pltpu.CompilerParams(dimension_semantics=sem, vmem_limit_bytes=VMEM_LIMIT)


def _rms(x, g):
    ms = jnp.mean(x * x, axis=-1, keepdims=True)
    return x * lax.rsqrt(ms + EPS) * g


def _dot(a, b):
    return jnp.dot(a, b, preferred_element_type=F32)


def _dot_nt(a, b):
    return lax.dot_general(a, b, (((1,), (1,)), ((), ())), preferred_element_type=F32)


def _ffn_body(x_ref, g_ref, wg_ref, wu_ref, wd_ref, *rest, final):
    o_ref = rest[-1]
    x = x_ref[...]
    xn = _rms(x, g_ref[...]).astype(BF16)
    acc = None
    for lo, hi in zip(FF_SPLITS[:-1], FF_SPLITS[1:]):
        sl = slice(lo, hi)
        gate = _dot(xn, wg_ref[:, sl])
        up = _dot(xn, wu_ref[:, sl])
        h = (gate * jax.nn.sigmoid(gate) * up).astype(BF16)
        y = _dot(h, wd_ref[sl, :])
        acc = y if acc is None else acc + y
    out = x + 0.5 * acc
    if final:
        out = _rms(out, rest[0][...])
    o_ref[...] = out


def _ffn(x, g, wg, wu, wd, final_g=None):
    m = x.shape[0]
    row = pl.BlockSpec((TM, D_MODEL), lambda i: (i, 0))
    in_specs = [row, _const_spec((1, D_MODEL)), _const_spec((D_MODEL, D_FF)),
                _const_spec((D_MODEL, D_FF)), _const_spec((D_FF, D_MODEL))]
    args = [x, g, wg, wu, wd]
    if final_g is not None:
        in_specs.append(_const_spec((1, D_MODEL)))
        args.append(final_g)
    return pl.pallas_call(
        functools.partial(_ffn_body, final=final_g is not None),
        grid=(m // TM,), in_specs=in_specs, out_specs=row,
        out_shape=jax.ShapeDtypeStruct((m, D_MODEL), F32),
        compiler_params=_params("parallel"), name="ffn")(*args)


def _even_in_body(x_ref, g_ref, win_ref, qg_ref, kvg_ref, wqa_ref, wqb_ref, wuk_ref, wuv_ref, tab_ref,
                  bg_ref, u_ref, q_ref, k_ref, v_ref):
    xn = _rms(x_ref[...], g_ref[...]).astype(BF16)
    z = _dot(xn, win_ref[...])
    bg_ref[...] = z[:, :D_CONV]
    u_ref[...] = z[:, D_CONV:2 * D_CONV] * z[:, 2 * D_CONV:3 * D_CONV]
    qn = _rms(z[:, EV_QLAT:EV_KVLAT], qg_ref[...]).astype(BF16)
    kvn = _rms(z[:, EV_KVLAT:EV_KR], kvg_ref[...]).astype(BF16)
    cosk = tab_ref[:, 0:LANES]
    sink = tab_ref[:, LANES:2 * LANES]
    scale = (QK_NOPE + QK_ROPE) ** -0.5 * math.log2(math.e)
    nope = lax.broadcasted_iota(jnp.int32, (1, LANES), 1) < QK_NOPE
    cosq = scale * jnp.where(nope, 1.0, cosk)
    sinq = scale * sink
    kpe = z[:, EV_KR:EV_KR + LANES] * cosk + z[:, EV_KR + LANES:EV_KR + 2 * LANES] * sink
    qa = _dot(qn, wqa_ref[...])
    qb = _dot(qn, wqb_ref[...])
    kn = _dot(kvn, wuk_ref[...])
    lane = lax.broadcasted_iota(jnp.int32, (1, MLA_HEADS * LANES), 1)
    ones_lanes = jnp.where(lane % LANES >= V_DIM, 1.0, 0.0)
    v_ref[...] = (_dot(kvn, wuv_ref[...]) + ones_lanes).astype(BF16)
    for h in range(MLA_HEADS):
        sl = slice(h * LANES, (h + 1) * LANES)
        q_ref[:, sl] = (qa[:, sl] * cosq + qb[:, sl] * sinq).astype(BF16)
        k_ref[:, sl] = (kn[:, sl] + kpe).astype(BF16)


def _even_in(x, g, w, tab, seq):
    m = x.shape[0]
    hw = MLA_HEADS * LANES
    tiles_per_seq = seq // TM
    row = lambda n: pl.BlockSpec((TM, n), lambda i: (i, 0))
    in_specs = [row(D_MODEL), _const_spec((1, D_MODEL)), _const_spec((D_MODEL, EV_IN_EXT)),
                _const_spec((1, Q_RANK)), _const_spec((1, KV_RANK)),
                _const_spec((Q_RANK, hw)), _const_spec((Q_RANK, hw)),
                _const_spec((KV_RANK, hw)), _const_spec((KV_RANK, hw)),
                pl.BlockSpec((TM, 2 * LANES), lambda i: (i % tiles_per_seq, 0))]
    out_specs = [row(D_CONV), row(D_CONV), row(hw), row(hw), row(hw)]
    out_shape = [jax.ShapeDtypeStruct((m, D_CONV), F32), jax.ShapeDtypeStruct((m, D_CONV), F32),
                 jax.ShapeDtypeStruct((m, hw), BF16), jax.ShapeDtypeStruct((m, hw), BF16),
                 jax.ShapeDtypeStruct((m, hw), BF16)]
    return pl.pallas_call(
        _even_in_body, grid=(m // TM,), in_specs=in_specs, out_specs=out_specs, out_shape=out_shape,
        compiler_params=_params("parallel"), name="even_in")(
            x, g, w["win"], w["q_norm"], w["kv_norm"], w["wqa"], w["wqb"], w["wuk"], w["wuv"], tab)


def _flash_body(q_ref, k_ref, v_ref, o_ref, m_ref, acc_ref, s_buf, p_buf):
    j = pl.program_id(2)

    @pl.when(j == 0)
    def _():
        m_ref[...] = jnp.full(m_ref.shape, -jnp.inf, F32)
        acc_ref[...] = jnp.zeros(acc_ref.shape, F32)

    units = [(h, r0) for h in range(MLA_HEADS) for r0 in range(0, TQ, FLASH_UNIT)]
    n_chunks = TK // FLASH_KEYS
    assert len(units) % FLASH_SLOTS == 0

    def scores(u, k0):
        h, r0 = units[u]
        sl = slice(h * LANES, (h + 1) * LANES)
        s_buf[u % FLASH_SLOTS] = _dot_nt(q_ref[r0:r0 + FLASH_UNIT, sl], k_ref[pl.ds(k0, FLASH_KEYS), sl])

    def chunk(c, carry):
        k0 = pl.multiple_of(c * FLASH_KEYS, FLASH_KEYS)
        k_next = pl.multiple_of(jnp.minimum(c + 1, n_chunks - 1) * FLASH_KEYS, FLASH_KEYS)
        for u, (h, r0) in enumerate(units):
            ahead = u + FLASH_AHEAD
            if ahead < len(units):
                scores(ahead, k0)
            else:
                scores(ahead - len(units), k_next)
            for r in range(0, FLASH_UNIT, FLASH_ROWS):
                rows = slice(r0 + r, r0 + r + FLASH_ROWS)
                s = s_buf[u % FLASH_SLOTS, r:r + FLASH_ROWS, :]
                m_prev = m_ref[h, rows, :]
                m_new = jnp.maximum(m_prev, jnp.max(s, axis=-1, keepdims=True))
                p_buf[u % 2, r:r + FLASH_ROWS, :] = jnp.exp2(s - m_new).astype(BF16)
                acc_ref[h, rows, :] = acc_ref[h, rows, :] * jnp.exp2(m_prev - m_new)
                m_ref[h, rows, :] = m_new
            acc_ref[h, r0:r0 + FLASH_UNIT, :] += _dot(p_buf[u % 2],
                                                      v_ref[pl.ds(k0, FLASH_KEYS), h * LANES:(h + 1) * LANES])
        return carry

    for u in range(FLASH_AHEAD):
        scores(u, 0)
    lax.fori_loop(0, n_chunks, chunk, 0)

    @pl.when(j == pl.num_programs(2) - 1)
    def _():
        low_half = lax.broadcasted_iota(jnp.int32, (TQ, LANES), 1) < V_DIM
        for pair in range(MLA_HEADS // 2):
            even = acc_ref[2 * pair]
            odd = acc_ref[2 * pair + 1]
            even_sw = pltpu.roll(even, V_DIM, 1)
            odd_sw = pltpu.roll(odd, V_DIM, 1)
            out = jnp.where(low_half, even / even_sw, odd_sw / odd)
            o_ref[:, pair * LANES:(pair + 1) * LANES] = out.astype(BF16)


def _flash(q, k, v, batch, seq):
    m = q.shape[0]
    hw = MLA_HEADS * LANES
    nq, nk = seq // TQ, seq // TK
    ow = MLA_HEADS * V_DIM
    return pl.pallas_call(
        _flash_body, grid=(batch, nq, nk),
        in_specs=[pl.BlockSpec((TQ, hw), lambda b, i, j: (b * nq + i, 0)),
                  pl.BlockSpec((TK, hw), lambda b, i, j: (b * nk + j, 0)),
                  pl.BlockSpec((TK, hw), lambda b, i, j: (b * nk + j, 0))],
        out_specs=pl.BlockSpec((TQ, ow), lambda b, i, j: (b * nq + i, 0)),
        out_shape=jax.ShapeDtypeStruct((m, ow), BF16),
        scratch_shapes=[pltpu.VMEM((MLA_HEADS, TQ, 1), F32), pltpu.VMEM((MLA_HEADS, TQ, LANES), F32),
                        pltpu.VMEM((FLASH_SLOTS, FLASH_UNIT, FLASH_KEYS), F32),
                        pltpu.VMEM((2, FLASH_UNIT, FLASH_KEYS), BF16)],
        compiler_params=pltpu.CompilerParams(dimension_semantics=("parallel", "parallel", "arbitrary"),
                                             vmem_limit_bytes=FLASH_VMEM_LIMIT), name="mla_flash")(q, k, v)


def _even_out_body(bg_ref, u_ref, up_ref, un_ref, cw_ref, yb_ref, wo_ref, x_ref, o_ref, *, tiles_per_seq):
    t = pl.program_id(0) % tiles_per_seq
    u = u_ref[...]
    row = lax.broadcasted_iota(jnp.int32, u.shape, 0)
    prev = jnp.where(t == 0, 0.0, up_ref[HALO - 1:HALO, :])
    nxt = jnp.where(t == tiles_per_seq - 1, 0.0, un_ref[0:1, :])
    um1 = jnp.where(row == 0, prev, pltpu.roll(u, 1, 0))
    up1 = jnp.where(row == TM - 1, nxt, pltpu.roll(u, TM - 1, 0))
    conv = um1 * cw_ref[0:1, :] + u * cw_ref[1:2, :] + up1 * cw_ref[2:3, :]
    ya = (bg_ref[...] * conv).astype(BF16)
    y = _dot(ya, wo_ref[0:D_CONV, :]) + _dot(yb_ref[...], wo_ref[D_CONV:, :])
    o_ref[...] = x_ref[...] + y


def _halo_specs(tile, width, n_rows):
    per = tile // HALO
    last = n_rows // HALO - 1
    prev = pl.BlockSpec((HALO, width), lambda i: (jnp.maximum(i * per - 1, 0), 0))
    nxt = pl.BlockSpec((HALO, width), lambda i: (jnp.minimum((i + 1) * per, last), 0))
    return prev, nxt


def _even_out(bg, u, conv_w, yb, wo, x, seq):
    m = x.shape[0]
    row = lambda n: pl.BlockSpec((TM, n), lambda i: (i, 0))
    prev, nxt = _halo_specs(TM, D_CONV, m)
    return pl.pallas_call(
        functools.partial(_even_out_body, tiles_per_seq=seq // TM), grid=(m // TM,),
        in_specs=[row(D_CONV), row(D_CONV), prev, nxt, _const_spec(conv_w.shape), row(MLA_HEADS * V_DIM),
                  _const_spec(wo.shape), row(D_MODEL)],
        out_specs=row(D_MODEL), out_shape=jax.ShapeDtypeStruct((m, D_MODEL), F32),
        compiler_params=_params("parallel"), name="even_out")(bg, u, u, u, conv_w, yb, wo, x)


def _odd_in_body(x_ref, xp_ref, xn_ref, g_ref, win_ref, cw_ref, cb_ref, xc_ref, gate_ref, q_ref, k_ref, v_ref, *,
                 tiles_per_seq):
    t = pl.program_id(0) % tiles_per_seq
    x_ext = jnp.concatenate([xp_ref[...], x_ref[...], xn_ref[...]], axis=0)
    z_ext = _dot(_rms(x_ext, g_ref[...]).astype(BF16), win_ref[...])
    z = z_ext[HALO:HALO + TM, :]
    xr = z_ext[:, :D_RNN]
    row = lax.broadcasted_iota(jnp.int32, xr.shape, 0)
    outside = ((row < HALO) & (t == 0)) | ((row >= HALO + TM) & (t == tiles_per_seq - 1))
    xr = jnp.where(outside, 0.0, xr)
    n_ext = TM + 2 * HALO
    xc = (pltpu.roll(xr, 2, 0) * cw_ref[0:1, :] + pltpu.roll(xr, 1, 0) * cw_ref[1:2, :] + xr * cw_ref[2:3, :]
          + pltpu.roll(xr, n_ext - 1, 0) * cw_ref[3:4, :])
    xc_ref[...] = xc[HALO:HALO + TM, :] + cb_ref[...]
    gate_ref[...] = z[:, D_RNN:OD_Q]
    q_ref[...] = (z[:, OD_Q:OD_K] * (SWA_HD ** -0.5)).astype(BF16)
    k_ref[...] = z[:, OD_K:OD_V].astype(BF16)
    lane = lax.broadcasted_iota(jnp.int32, (1, SWA_KV_HEADS * LANES), 1)
    v_ref[...] = (z[:, OD_V:] + jnp.where(lane % LANES >= SWA_HD, 1.0, 0.0)).astype(BF16)


def _odd_in(x, g, w, seq):
    m = x.shape[0]
    row = lambda n: pl.BlockSpec((TM, n), lambda i: (i, 0))
    prev, nxt = _halo_specs(TM, D_MODEL, m)
    widths = [D_RNN, D_RNN, SWA_HEADS * LANES, SWA_KV_HEADS * LANES, SWA_KV_HEADS * LANES]
    dtypes = [F32, F32, BF16, BF16, BF16]
    return pl.pallas_call(
        functools.partial(_odd_in_body, tiles_per_seq=seq // TM), grid=(m // TM,),
        in_specs=[row(D_MODEL), prev, nxt, _const_spec((1, D_MODEL)), _const_spec((D_MODEL, OD_IN_EXT)),
                  _const_spec(w["conv_w"].shape), _const_spec((1, D_RNN))],
        out_specs=[row(n) for n in widths],
        out_shape=[jax.ShapeDtypeStruct((m, n), d) for n, d in zip(widths, dtypes)],
        compiler_params=_params("parallel"), name="odd_in")(x, x, x, g, w["win"], w["conv_w"], w["conv_b"])


def _lru_body(x_ref, wa_ref, ba_ref, wx_ref, bx_ref, lam_ref, o_ref, a_s, u_s, c_s, carry_s, *, reverse):
    @pl.when(pl.program_id(1) == 0)
    def _():
        carry_s[...] = jnp.zeros(carry_s.shape, F32)

    xc = x_ref[...]
    xcb = xc.astype(BF16)
    tr = jnp.tanh(0.5 * (_dot(xcb, wa_ref[...]) + ba_ref[...]))
    ti = jnp.tanh(0.5 * (_dot(xcb, wx_ref[...]) + bx_ref[...]))
    nl = -lam_ref[...]
    softplus = jnp.maximum(nl, 0.0) + jnp.log1p(jnp.exp(-jnp.abs(nl)))
    log_a = (-0.5 * LRU_C * softplus) * (tr + 1.0)
    a = jnp.exp(log_a)
    th = jnp.tanh(log_a)
    u = jnp.sqrt(-0.5 * th / (1.0 - th)) * ((ti + 1.0) * xc)
    n_groups = TS // SUBLANES
    a = a.reshape(n_groups, SUBLANES, D_RNN)
    u = u.reshape(n_groups, SUBLANES, D_RNN)
    in_group = lax.broadcasted_iota(jnp.int32, a.shape, 1)
    for d in (1, 2, 4):
        if reverse:
            keep = in_group < SUBLANES - d
            shift = SUBLANES - d
        else:
            keep = in_group >= d
            shift = d
        a_sh = jnp.where(keep, pltpu.roll(a, shift, 1), 1.0)
        u_sh = jnp.where(keep, pltpu.roll(u, shift, 1), 0.0)
        u = a * u_sh + u
        a = a * a_sh
    a_s[...] = a.reshape(TS, D_RNN)
    u_s[...] = u.reshape(TS, D_RNN)
    c = carry_s[...]
    for g in (range(n_groups - 1, -1, -1) if reverse else range(n_groups)):
        c_s[g:g + 1, :] = c
        e = g * SUBLANES + (0 if reverse else SUBLANES - 1)
        c = a_s[e:e + 1, :] * c + u_s[e:e + 1, :]
    carry_s[...] = c
    for g in range(n_groups):
        rows = slice(g * SUBLANES, (g + 1) * SUBLANES)
        o_ref[rows, :] = u_s[rows, :] + a_s[rows, :] * c_s[g:g + 1, :]


def _lru(xc, w, d, batch, seq, reverse):
    m = xc.shape[0]
    n_tiles = seq // TS
    main = pl.BlockSpec((TS, D_RNN), lambda b, i: (b * n_tiles + ((n_tiles - 1 - i) if reverse else i), 0))
    vec = _const_spec((1, D_RNN))
    mat = _const_spec((D_RNN, D_RNN))
    big = pltpu.VMEM((TS, D_RNN), F32)
    return pl.pallas_call(
        functools.partial(_lru_body, reverse=reverse), grid=(batch, n_tiles),
        in_specs=[main, mat, vec, mat, vec, vec],
        out_specs=main, out_shape=jax.ShapeDtypeStruct((m, D_RNN), F32),
        scratch_shapes=[big, big, pltpu.VMEM((TS // SUBLANES, D_RNN), F32), pltpu.VMEM((1, D_RNN), F32)],
        compiler_params=_params("arbitrary", "arbitrary"), name="lru_bwd" if reverse else "lru_fwd")(
            xc, w["wa"][d], w["ba"][d], w["wx"][d], w["bx"][d], w["lam"][d])


def _odd_out_body(sink_ref, q_ref, k_ref, kp_ref, kn_ref, v_ref, vp_ref, vn_ref, gate_ref, hf_ref, hb_ref,
                  wo_ref, x_ref, o_ref, kbuf, vbuf, s_buf, p_buf, yd_s, *, tiles_per_seq, seq):
    t = pl.program_id(0) % tiles_per_seq
    kbuf[0:BLOCK, :] = kp_ref[...]
    kbuf[BLOCK:BLOCK + TW, :] = k_ref[...]
    kbuf[BLOCK + TW:, :] = kn_ref[...]
    vbuf[0:BLOCK, :] = vp_ref[...]
    vbuf[BLOCK:BLOCK + TW, :] = v_ref[...]
    vbuf[BLOCK + TW:, :] = vn_ref[...]

    qi = lax.broadcasted_iota(jnp.int32, (BLOCK, 3 * BLOCK), 0)
    ci = lax.broadcasted_iota(jnp.int32, (BLOCK, 3 * BLOCK), 1)
    rel = jnp.abs(ci - BLOCK - qi)
    relf = rel.astype(F32)
    in_window = rel <= WINDOW
    low_half = lax.broadcasted_iota(jnp.int32, (BLOCK, LANES), 1) < SWA_HD

    units = [(jb, g) for jb in range(TW // BLOCK) for g in range(SWA_KV_HEADS)]

    def scores(u):
        jb, g = units[u]
        qs = jnp.concatenate([q_ref[jb * BLOCK:(jb + 1) * BLOCK, h * LANES:(h + 1) * LANES]
                              for h in range(g * SWA_GROUP, (g + 1) * SWA_GROUP)], axis=0)
        s_buf[u % 2] = _dot_nt(qs, kbuf[jb * BLOCK:(jb + 3) * BLOCK, g * LANES:(g + 1) * LANES])

    scores(0)
    for u, (jb, g) in enumerate(units):
        if u + 1 < len(units):
            scores(u + 1)
        key_pos = t * TW + (jb - 1) * BLOCK + ci
        valid = in_window & (key_pos >= 0) & (key_pos < seq)
        sink_terms = []
        for hh in range(SWA_GROUP):
            h = g * SWA_GROUP + hh
            rows = slice(hh * BLOCK, (hh + 1) * BLOCK)
            s = jnp.where(valid, s_buf[u % 2, rows, :] - (2.0 ** -(h + 1)) * relf, -jnp.inf)
            sk = sink_ref[h]
            mx = jnp.maximum(jnp.max(s, axis=-1, keepdims=True), sk)
            p_buf[u % 2, rows, :] = jnp.exp(s - mx).astype(BF16)
            sink_terms.append(jnp.exp(sk - mx))
        o = _dot(p_buf[u % 2], vbuf[jb * BLOCK:(jb + 3) * BLOCK, g * LANES:(g + 1) * LANES])
        for pp in range(SWA_GROUP // 2):
            oe = o[2 * pp * BLOCK:(2 * pp + 1) * BLOCK, :]
            oo = o[(2 * pp + 1) * BLOCK:(2 * pp + 2) * BLOCK, :]
            ye = oe / (pltpu.roll(oe, SWA_HD, 1) + sink_terms[2 * pp])
            yo = pltpu.roll(oo, SWA_HD, 1) / (oo + sink_terms[2 * pp + 1])
            pair = (g * SWA_GROUP) // 2 + pp
            yd_s[jb * BLOCK:(jb + 1) * BLOCK, pair * LANES:(pair + 1) * LANES] = (
                jnp.where(low_half, ye, yo).astype(BF16))

    yc = (jax.nn.gelu(gate_ref[...]) * (hf_ref[...] + hb_ref[...])).astype(BF16)
    y = _dot(yc, wo_ref[0:D_RNN, :]) + _dot(yd_s[...], wo_ref[D_RNN:, :])
    o_ref[...] = x_ref[...] + y


def _odd_out(sink, q, k, v, gate, hf, hb, wo, x, seq):
    m = x.shape[0]
    kw = SWA_KV_HEADS * LANES
    vw = SWA_KV_HEADS * LANES
    unit = (SWA_GROUP * BLOCK, 3 * BLOCK)
    per = TW // BLOCK
    last = m // BLOCK - 1
    row = lambda n: pl.BlockSpec((TW, n), lambda i: (i, 0))
    prev = lambda n: pl.BlockSpec((BLOCK, n), lambda i: (jnp.maximum(i * per - 1, 0), 0))
    nxt = lambda n: pl.BlockSpec((BLOCK, n), lambda i: (jnp.minimum((i + 1) * per, last), 0))
    return pl.pallas_call(
        functools.partial(_odd_out_body, tiles_per_seq=seq // TW, seq=seq), grid=(m // TW,),
        in_specs=[pl.BlockSpec(memory_space=pltpu.SMEM),
                  row(SWA_HEADS * LANES), row(kw), prev(kw), nxt(kw), row(vw), prev(vw), nxt(vw),
                  row(D_RNN), row(D_RNN), row(D_RNN), _const_spec(wo.shape), row(D_MODEL)],
        out_specs=row(D_MODEL), out_shape=jax.ShapeDtypeStruct((m, D_MODEL), F32),
        scratch_shapes=[pltpu.VMEM((TW + 2 * BLOCK, kw), BF16), pltpu.VMEM((TW + 2 * BLOCK, vw), BF16),
                        pltpu.VMEM((2,) + unit, F32), pltpu.VMEM((2,) + unit, BF16),
                        pltpu.VMEM((TW, SWA_HEADS * SWA_HD), BF16)],
        compiler_params=_params("parallel"), name="odd_out")(
            sink, q, k, k, k, v, v, v, gate, hf, hb, wo, x)


def _rotate_half_cols(w):
    half = w.shape[-1] // 2
    return jnp.concatenate([-w[..., half:], w[..., :half]], axis=-1)


def _even_weights(w_in, conv_w, q_norm, w_uq, kv_norm, w_ukv, w_out):
    zc = lambda n: jnp.zeros((D_MODEL, n), F32)
    kr = w_in[:, EV_KR:]
    win = jnp.concatenate([w_in[:, :EV_KR], zc(QK_NOPE), kr, zc(LANES - QK_NOPE - QK_ROPE),
                           zc(QK_NOPE), _rotate_half_cols(kr), zc(LANES - QK_NOPE - QK_ROPE)], axis=1)
    wq = w_uq.reshape(Q_RANK, MLA_HEADS, QK_NOPE + QK_ROPE)
    nope, pe = wq[..., :QK_NOPE], wq[..., QK_NOPE:]
    zq = lambda n: jnp.zeros((Q_RANK, MLA_HEADS, n), F32)
    wqa = jnp.concatenate([nope, pe, zq(LANES - QK_NOPE - QK_ROPE)], -1).reshape(Q_RANK, MLA_HEADS * LANES)
    wqb = jnp.concatenate([zq(QK_NOPE), _rotate_half_cols(pe), zq(LANES - QK_NOPE - QK_ROPE)], -1)
    wqb = wqb.reshape(Q_RANK, MLA_HEADS * LANES)
    wkv = w_ukv.reshape(KV_RANK, MLA_HEADS, QK_NOPE + V_DIM)
    wuk = jnp.concatenate([wkv[..., :QK_NOPE], jnp.zeros((KV_RANK, MLA_HEADS, LANES - QK_NOPE), F32)], -1)
    wuv = jnp.concatenate([wkv[..., QK_NOPE:], jnp.zeros((KV_RANK, MLA_HEADS, LANES - V_DIM), F32)], -1)
    return dict(win=win.astype(BF16), conv_w=conv_w, q_norm=q_norm[None, :], kv_norm=kv_norm[None, :],
                wqa=wqa.astype(BF16), wqb=wqb.astype(BF16),
                wuk=wuk.reshape(KV_RANK, MLA_HEADS * LANES).astype(BF16),
                wuv=wuv.reshape(KV_RANK, MLA_HEADS * LANES).astype(BF16), wo=w_out.astype(BF16))


def _block_diag(w):
    eye = jnp.eye(LRU_BLOCKS, dtype=w.dtype)
    return jnp.einsum('ncd,nm->ncmd', w, eye).reshape(D_RNN, D_RNN)


def _odd_weights(w_in, conv_w, conv_b, w_a, b_a, w_x, b_x, lam, sink, w_out):
    qd = SWA_HEADS * SWA_HD
    kd = SWA_KV_HEADS * SWA_HD
    pad = lambda w, heads: jnp.concatenate(
        [w.reshape(D_MODEL, heads, SWA_HD), jnp.zeros((D_MODEL, heads, LANES - SWA_HD), F32)], -1
    ).reshape(D_MODEL, heads * LANES)
    wq = w_in[:, OD_Q:OD_Q + qd]
    wk = w_in[:, OD_Q + qd:OD_Q + qd + kd]
    wv = w_in[:, OD_Q + qd + kd:]
    win = jnp.concatenate([w_in[:, :OD_Q], pad(wq, SWA_HEADS), pad(wk, SWA_KV_HEADS), pad(wv, SWA_KV_HEADS)],
                          axis=1)
    return dict(win=win.astype(BF16), conv_w=conv_w, conv_b=conv_b[None, :],
                wa=[_block_diag(w_a[d]).astype(BF16) for d in range(2)], ba=[b_a[d][None, :] for d in range(2)],
                wx=[_block_diag(w_x[d]).astype(BF16) for d in range(2)], bx=[b_x[d][None, :] for d in range(2)],
                lam=[lam[d][None, :] for d in range(2)], sink=sink, wo=w_out.astype(BF16))


def _rope_tables(seq):
    half = QK_ROPE // 2
    inv = ROPE_THETA ** (-jnp.arange(half, dtype=F32) / half)
    ang = jnp.arange(seq).astype(F32)[:, None] * inv[None, :]
    z_lo = jnp.zeros((seq, QK_NOPE), F32)
    z_hi = jnp.zeros((seq, LANES - QK_NOPE - QK_ROPE), F32)
    cos = jnp.concatenate([z_lo, jnp.cos(ang), jnp.cos(ang), z_hi], 1)
    sin = jnp.concatenate([z_lo, jnp.sin(ang), jnp.sin(ang), z_hi], 1)
    return jnp.concatenate([cos, sin], 1)


def _trunk(x3, tab, ffn, mix_norm, even, odd, final_norm):
    batch, seq, _ = x3.shape
    x = x3.reshape(batch * seq, D_MODEL)
    depth = len(ffn)
    for l in range(depth):
        x = _ffn(x, *ffn[l][0])
        g = mix_norm[l][None, :]
        if l % 2 == 0:
            w = even[l // 2]
            bg, u, q, k, v = _even_in(x, g, w, tab, seq)
            yb = _flash(q, k, v, batch, seq)
            x = _even_out(bg, u, w["conv_w"], yb, w["wo"], x, seq)
        else:
            w = odd[l // 2]
            xc, gate, q, k, v = _odd_in(x, g, w, seq)
            hf = _lru(xc, w, 0, batch, seq, reverse=False)
            hb = _lru(xc, w, 1, batch, seq, reverse=True)
            x = _odd_out(w["sink"], q, k, v, gate, hf, hb, w["wo"], x, seq)
        x = _ffn(x, *ffn[l][1], final_g=final_norm[None, :] if l == depth - 1 else None)
    return x.reshape(batch, seq, D_MODEL)


def kernel(x_prompt, x_sample, ffn_norm, ffn_w_gate, ffn_w_up, ffn_w_down, mix_norm, ev_w_in, ev_conv_w, mla_q_norm, mla_w_uq, mla_kv_norm, mla_w_ukv, ev_w_out, od_w_in, od_conv_w, od_conv_b, lru_w_a, lru_b_a, lru_w_x, lru_b_x, lru_lambda, swa_sink, od_w_out, final_norm):
    depth = ffn_norm.shape[0]
    ffn = [[(ffn_norm[l, s][None, :], ffn_w_gate[l, s].astype(BF16), ffn_w_up[l, s].astype(BF16),
             ffn_w_down[l, s].astype(BF16)) for s in range(2)] for l in range(depth)]
    even = [_even_weights(ev_w_in[j], ev_conv_w[j], mla_q_norm[j], mla_w_uq[j], mla_kv_norm[j], mla_w_ukv[j],
                          ev_w_out[j]) for j in range(ev_w_in.shape[0])]
    odd = [_odd_weights(od_w_in[j], od_conv_w[j], od_conv_b[j], lru_w_a[j], lru_b_a[j], lru_w_x[j], lru_b_x[j],
                        lru_lambda[j], swa_sink[j], od_w_out[j]) for j in range(od_w_in.shape[0])]
    tab = _rope_tables(max(x_prompt.shape[1], x_sample.shape[1]))
    y_prompt = _trunk(x_prompt, tab, ffn, mix_norm, even, odd, final_norm)
    y_sample = _trunk(x_sample, tab, ffn, mix_norm, even, odd, final_norm)
    return (y_prompt, y_sample)
```

```python
import functools
import math

import jax
import jax.numpy as jnp
from jax import lax
from jax.experimental import pallas as pl
from jax.experimental.pallas import tpu as pltpu

F32 = jnp.float32
BF16 = jnp.bfloat16

EPS = 1e-6
D_MODEL = 1024
D_FF = 2816
BLOCK = 128
LANES = 128
SUBLANES = 8

D_CONV = 512
MLA_HEADS = 8
QK_NOPE = 64
QK_ROPE = 32
V_DIM = 64
Q_RANK = 384
KV_RANK = 256
ROPE_THETA = 10000.0

D_RNN = 512
LRU_BLOCKS = 8
LRU_BW = D_RNN // LRU_BLOCKS
LRU_C = 8.0

SWA_HEADS = 8
SWA_KV_HEADS = 2
SWA_HD = 64
SWA_GROUP = SWA_HEADS // SWA_KV_HEADS
WINDOW = 128

TM = 512
MXU_TILE = 256
FF_SPLITS = (0, 4 * MXU_TILE, 8 * MXU_TILE, D_FF)
TM_FFN = 1024
TQ = 1024
FLASH_UNIT = 512
TK = 2048
FLASH_KEYS = 1024
FLASH_AHEAD = 2
FLASH_SLOTS = 4
FLASH_ROWS = 32
TS = 512
TW = 512
HALO = SUBLANES

VMEM_LIMIT = 56 * 1024 * 1024
FLASH_VMEM_LIMIT = VMEM_LIMIT

EV_QLAT = 3 * D_CONV
EV_KVLAT = EV_QLAT + Q_RANK
EV_KR = EV_KVLAT + KV_RANK
EV_IN_EXT = EV_KR + 2 * LANES
OD_Q = 2 * D_RNN
OD_K = OD_Q + SWA_HEADS * LANES
OD_V = OD_K + SWA_KV_HEADS * LANES
OD_IN_EXT = OD_V + SWA_KV_HEADS * LANES


def _const_spec(shape):
    zeros = (0,) * len(shape)
    return pl.BlockSpec(shape, lambda *_: zeros, pipeline_mode=pl.Buffered(1))


def _params(*sem):
    return pltpu.CompilerParams(dimension_semantics=sem, vmem_limit_bytes=VMEM_LIMIT)


def _rms(x, g):
    ms = jnp.mean(x * x, axis=-1, keepdims=True)
    return x * lax.rsqrt(ms + EPS) * g


def _dot(a, b):
    return jnp.dot(a, b, preferred_element_type=F32)


def _dot_nt(a, b):
    return lax.dot_general(a, b, (((1,), (1,)), ((), ())), preferred_element_type=F32)


def _ffn_body(x_ref, g_ref, wg_ref, wu_ref, wd_ref, *rest, final):
    o_ref = rest[-1]
    x = x_ref[...]
    xn = _rms(x, g_ref[...]).astype(BF16)
    acc = None
    for lo, hi in zip(FF_SPLITS[:-1], FF_SPLITS[1:]):
        sl = slice(lo, hi)
        gate = _dot(xn, wg_ref[:, sl])
        up = _dot(xn, wu_ref[:, sl])
        h = (gate * jax.nn.sigmoid(gate) * up).astype(BF16)
        y = _dot(h, wd_ref[sl, :])
        acc = y if acc is None else acc + y
    out = x + 0.5 * acc
    if final:
        out = _rms(out, rest[0][...])
    o_ref[...] = out


def _ffn(x, g, wg, wu, wd, final_g=None):
    m = x.shape[0]
    row = pl.BlockSpec((TM_FFN, D_MODEL), lambda i: (i, 0))
    in_specs = [row, _const_spec((1, D_MODEL)), _const_spec((D_MODEL, D_FF)),
                _const_spec((D_MODEL, D_FF)), _const_spec((D_FF, D_MODEL))]
    args = [x, g, wg, wu, wd]
    if final_g is not None:
        in_specs.append(_const_spec((1, D_MODEL)))
        args.append(final_g)
    return pl.pallas_call(
        functools.partial(_ffn_body, final=final_g is not None),
        grid=(m // TM_FFN,), in_specs=in_specs, out_specs=row,
        out_shape=jax.ShapeDtypeStruct((m, D_MODEL), F32),
        compiler_params=_params("parallel"), name="ffn")(*args)


def _even_in_body(x_ref, g_ref, win_ref, qg_ref, kvg_ref, wqa_ref, wqb_ref, wuk_ref, wuv_ref, tab_ref,
                  bg_ref, u_ref, q_ref, k_ref, v_ref):
    xn = _rms(x_ref[...], g_ref[...]).astype(BF16)
    z = _dot(xn, win_ref[...])
    bg_ref[...] = z[:, :D_CONV]
    u_ref[...] = z[:, D_CONV:2 * D_CONV] * z[:, 2 * D_CONV:3 * D_CONV]
    qn = _rms(z[:, EV_QLAT:EV_KVLAT], qg_ref[...]).astype(BF16)
    kvn = _rms(z[:, EV_KVLAT:EV_KR], kvg_ref[...]).astype(BF16)
    cosk = tab_ref[:, 0:LANES]
    sink = tab_ref[:, LANES:2 * LANES]
    scale = (QK_NOPE + QK_ROPE) ** -0.5 * math.log2(math.e)
    nope = lax.broadcasted_iota(jnp.int32, (1, LANES), 1) < QK_NOPE
    cosq = scale * jnp.where(nope, 1.0, cosk)
    sinq = scale * sink
    kpe = z[:, EV_KR:EV_KR + LANES] * cosk + z[:, EV_KR + LANES:EV_KR + 2 * LANES] * sink
    qa = _dot(qn, wqa_ref[...])
    qb = _dot(qn, wqb_ref[...])
    kn = _dot(kvn, wuk_ref[...])
    lane = lax.broadcasted_iota(jnp.int32, (1, MLA_HEADS * LANES), 1)
    ones_lanes = jnp.where(lane % LANES >= V_DIM, 1.0, 0.0)
    v_ref[...] = (_dot(kvn, wuv_ref[...]) + ones_lanes).astype(BF16)
    for h in range(MLA_HEADS):
        sl = slice(h * LANES, (h + 1) * LANES)
        q_ref[:, sl] = (qa[:, sl] * cosq + qb[:, sl] * sinq).astype(BF16)
        k_ref[:, sl] = (kn[:, sl] + kpe).astype(BF16)


def _even_in(x, g, w, tab, seq):
    m = x.shape[0]
    hw = MLA_HEADS * LANES
    tiles_per_seq = seq // TM
    row = lambda n: pl.BlockSpec((TM, n), lambda i: (i, 0))
    in_specs = [row(D_MODEL), _const_spec((1, D_MODEL)), _const_spec((D_MODEL, EV_IN_EXT)),
                _const_spec((1, Q_RANK)), _const_spec((1, KV_RANK)),
                _const_spec((Q_RANK, hw)), _const_spec((Q_RANK, hw)),
                _const_spec((KV_RANK, hw)), _const_spec((KV_RANK, hw)),
                pl.BlockSpec((TM, 2 * LANES), lambda i: (i % tiles_per_seq, 0))]
    out_specs = [row(D_CONV), row(D_CONV), row(hw), row(hw), row(hw)]
    out_shape = [jax.ShapeDtypeStruct((m, D_CONV), F32), jax.ShapeDtypeStruct((m, D_CONV), F32),
                 jax.ShapeDtypeStruct((m, hw), BF16), jax.ShapeDtypeStruct((m, hw), BF16),
                 jax.ShapeDtypeStruct((m, hw), BF16)]
    return pl.pallas_call(
        _even_in_body, grid=(m // TM,), in_specs=in_specs, out_specs=out_specs, out_shape=out_shape,
        compiler_params=_params("parallel"), name="even_in")(
            x, g, w["win"], w["q_norm"], w["kv_norm"], w["wqa"], w["wqb"], w["wuk"], w["wuv"], tab)


def _flash_body(q_ref, k_ref, v_ref, o_ref, m_ref, acc_ref, s_buf, p_buf, *, n_kv):
    j = pl.program_id(2)

    @pl.when(j == 0)
    def _():
        m_ref[...] = jnp.full(m_ref.shape, -jnp.inf, F32)
        acc_ref[...] = jnp.zeros(acc_ref.shape, F32)

    units = [(h, r0) for h in range(MLA_HEADS) for r0 in range(0, TQ, FLASH_UNIT)]
    n_chunks = TK // FLASH_KEYS
    assert len(units) % FLASH_SLOTS == 0

    def scores(u, k0):
        h, r0 = units[u]
        sl = slice(h * LANES, (h + 1) * LANES)
        s_buf[u % FLASH_SLOTS] = _dot_nt(q_ref[r0:r0 + FLASH_UNIT, sl], k_ref[pl.ds(k0, FLASH_KEYS), sl])

    def chunk(c, carry):
        k0 = pl.multiple_of(c * FLASH_KEYS, FLASH_KEYS)
        k_next = pl.multiple_of(jnp.minimum(c + 1, n_chunks - 1) * FLASH_KEYS, FLASH_KEYS)
        for u, (h, r0) in enumerate(units):
            ahead = u + FLASH_AHEAD
            if ahead < len(units):
                scores(ahead, k0)
            else:
                scores(ahead - len(units), k_next)
            for r in range(0, FLASH_UNIT, FLASH_ROWS):
                rows = slice(r0 + r, r0 + r + FLASH_ROWS)
                s = s_buf[u % FLASH_SLOTS, r:r + FLASH_ROWS, :]
                m_prev = m_ref[h, rows, :]
                m_new = jnp.maximum(m_prev, jnp.max(s, axis=-1, keepdims=True))
                p_buf[u % 2, r:r + FLASH_ROWS, :] = jnp.exp2(s - m_new).astype(BF16)
                acc_ref[h, rows, :] = acc_ref[h, rows, :] * jnp.exp2(m_prev - m_new)
                m_ref[h, rows, :] = m_new
            acc_ref[h, r0:r0 + FLASH_UNIT, :] += _dot(p_buf[u % 2],
                                                      v_ref[pl.ds(k0, FLASH_KEYS), h * LANES:(h + 1) * LANES])
        return carry

    for u in range(FLASH_AHEAD):
        scores(u, 0)
    lax.fori_loop(0, n_chunks, chunk, 0)

    @pl.when(j == n_kv - 1)
    def _():
        low_half = lax.broadcasted_iota(jnp.int32, (TQ, LANES), 1) < V_DIM
        for pair in range(MLA_HEADS // 2):
            even = acc_ref[2 * pair]
            odd = acc_ref[2 * pair + 1]
            even_sw = pltpu.roll(even, V_DIM, 1)
            odd_sw = pltpu.roll(odd, V_DIM, 1)
            out = jnp.where(low_half, even / even_sw, odd_sw / odd)
            o_ref[:, pair * LANES:(pair + 1) * LANES] = out.astype(BF16)


def _flash(q, k, v, batch, seq):
    m = q.shape[0]
    hw = MLA_HEADS * LANES
    nq, nk = seq // TQ, seq // TK
    ow = MLA_HEADS * V_DIM
    return pl.pallas_call(
        functools.partial(_flash_body, n_kv=nk), grid=(batch, nq, nk),
        in_specs=[pl.BlockSpec((TQ, hw), lambda b, i, j: (b * nq + i, 0)),
                  pl.BlockSpec((TK, hw), lambda b, i, j: (b * nk + j, 0)),
                  pl.BlockSpec((TK, hw), lambda b, i, j: (b * nk + j, 0))],
        out_specs=pl.BlockSpec((TQ, ow), lambda b, i, j: (b * nq + i, 0)),
        out_shape=jax.ShapeDtypeStruct((m, ow), BF16),
        scratch_shapes=[pltpu.VMEM((MLA_HEADS, TQ, 1), F32), pltpu.VMEM((MLA_HEADS, TQ, LANES), F32),
                        pltpu.VMEM((FLASH_SLOTS, FLASH_UNIT, FLASH_KEYS), F32),
                        pltpu.VMEM((2, FLASH_UNIT, FLASH_KEYS), BF16)],
        compiler_params=pltpu.CompilerParams(dimension_semantics=("parallel", "parallel", "arbitrary"),
                                             vmem_limit_bytes=FLASH_VMEM_LIMIT), name="mla_flash")(q, k, v)


def _even_out_body(bg_ref, u_ref, up_ref, un_ref, cw_ref, yb_ref, wo_ref, x_ref, o_ref, *, tiles_per_seq):
    t = pl.program_id(0) % tiles_per_seq
    u = u_ref[...]
    row = lax.broadcasted_iota(jnp.int32, u.shape, 0)
    prev = jnp.where(t == 0, 0.0, up_ref[HALO - 1:HALO, :])
    nxt = jnp.where(t == tiles_per_seq - 1, 0.0, un_ref[0:1, :])
    um1 = jnp.where(row == 0, prev, pltpu.roll(u, 1, 0))
    up1 = jnp.where(row == TM - 1, nxt, pltpu.roll(u, TM - 1, 0))
    conv = um1 * cw_ref[0:1, :] + u * cw_ref[1:2, :] + up1 * cw_ref[2:3, :]
    ya = (bg_ref[...] * conv).astype(BF16)
    y = _dot(ya, wo_ref[0:D_CONV, :]) + _dot(yb_ref[...], wo_ref[D_CONV:, :])
    o_ref[...] = x_ref[...] + y


def _halo_specs(tile, width, n_rows):
    per = tile // HALO
    last = n_rows // HALO - 1
    prev = pl.BlockSpec((HALO, width), lambda i: (jnp.maximum(i * per - 1, 0), 0))
    nxt = pl.BlockSpec((HALO, width), lambda i: (jnp.minimum((i + 1) * per, last), 0))
    return prev, nxt


def _even_out(bg, u, conv_w, yb, wo, x, seq):
    m = x.shape[0]
    row = lambda n: pl.BlockSpec((TM, n), lambda i: (i, 0))
    prev, nxt = _halo_specs(TM, D_CONV, m)
    return pl.pallas_call(
        functools.partial(_even_out_body, tiles_per_seq=seq // TM), grid=(m // TM,),
        in_specs=[row(D_CONV), row(D_CONV), prev, nxt, _const_spec(conv_w.shape), row(MLA_HEADS * V_DIM),
                  _const_spec(wo.shape), row(D_MODEL)],
        out_specs=row(D_MODEL), out_shape=jax.ShapeDtypeStruct((m, D_MODEL), F32),
        compiler_params=_params("parallel"), name="even_out")(bg, u, u, u, conv_w, yb, wo, x)


def _odd_in_body(x_ref, xp_ref, xn_ref, g_ref, win_ref, cw_ref, cb_ref, xc_ref, gate_ref, q_ref, k_ref, v_ref, *,
                 tiles_per_seq):
    t = pl.program_id(0) % tiles_per_seq
    x_ext = jnp.concatenate([xp_ref[...], x_ref[...], xn_ref[...]], axis=0)
    z_ext = _dot(_rms(x_ext, g_ref[...]).astype(BF16), win_ref[...])
    z = z_ext[HALO:HALO + TM, :]
    xr = z_ext[:, :D_RNN]
    row = lax.broadcasted_iota(jnp.int32, xr.shape, 0)
    outside = ((row < HALO) & (t == 0)) | ((row >= HALO + TM) & (t == tiles_per_seq - 1))
    xr = jnp.where(outside, 0.0, xr)
    n_ext = TM + 2 * HALO
    xc = (pltpu.roll(xr, 2, 0) * cw_ref[0:1, :] + pltpu.roll(xr, 1, 0) * cw_ref[1:2, :] + xr * cw_ref[2:3, :]
          + pltpu.roll(xr, n_ext - 1, 0) * cw_ref[3:4, :])
    xc_ref[...] = xc[HALO:HALO + TM, :] + cb_ref[...]
    gate_ref[...] = z[:, D_RNN:OD_Q]
    q_ref[...] = (z[:, OD_Q:OD_K] * (SWA_HD ** -0.5)).astype(BF16)
    k_ref[...] = z[:, OD_K:OD_V].astype(BF16)
    lane = lax.broadcasted_iota(jnp.int32, (1, SWA_KV_HEADS * LANES), 1)
    v_ref[...] = (z[:, OD_V:] + jnp.where(lane % LANES >= SWA_HD, 1.0, 0.0)).astype(BF16)


def _odd_in(x, g, w, seq):
    m = x.shape[0]
    row = lambda n: pl.BlockSpec((TM, n), lambda i: (i, 0))
    prev, nxt = _halo_specs(TM, D_MODEL, m)
    widths = [D_RNN, D_RNN, SWA_HEADS * LANES, SWA_KV_HEADS * LANES, SWA_KV_HEADS * LANES]
    dtypes = [F32, F32, BF16, BF16, BF16]
    return pl.pallas_call(
        functools.partial(_odd_in_body, tiles_per_seq=seq // TM), grid=(m // TM,),
        in_specs=[row(D_MODEL), prev, nxt, _const_spec((1, D_MODEL)), _const_spec((D_MODEL, OD_IN_EXT)),
                  _const_spec(w["conv_w"].shape), _const_spec((1, D_RNN))],
        out_specs=[row(n) for n in widths],
        out_shape=[jax.ShapeDtypeStruct((m, n), d) for n, d in zip(widths, dtypes)],
        compiler_params=_params("parallel"), name="odd_in")(x, x, x, g, w["win"], w["conv_w"], w["conv_b"])


def _lru_body(x_ref, wa_ref, ba_ref, wx_ref, bx_ref, lam_ref, o_ref, a_s, u_s, c_s, carry_s, *, reverse):
    @pl.when(pl.program_id(1) == 0)
    def _():
        carry_s[...] = jnp.zeros(carry_s.shape, F32)

    xc = x_ref[...]
    xcb = xc.astype(BF16)
    tr = jnp.tanh(0.5 * (_dot(xcb, wa_ref[...]) + ba_ref[...]))
    ti = jnp.tanh(0.5 * (_dot(xcb, wx_ref[...]) + bx_ref[...]))
    nl = -lam_ref[...]
    softplus = jnp.maximum(nl, 0.0) + jnp.log1p(jnp.exp(-jnp.abs(nl)))
    log_a = (-0.5 * LRU_C * softplus) * (tr + 1.0)
    a = jnp.exp(log_a)
    th = jnp.tanh(log_a)
    u = jnp.sqrt(-0.5 * th / (1.0 - th)) * ((ti + 1.0) * xc)
    n_groups = TS // SUBLANES
    a = a.reshape(n_groups, SUBLANES, D_RNN)
    u = u.reshape(n_groups, SUBLANES, D_RNN)
    in_group = lax.broadcasted_iota(jnp.int32, a.shape, 1)
    for d in (1, 2, 4):
        if reverse:
            keep = in_group < SUBLANES - d
            shift = SUBLANES - d
        else:
            keep = in_group >= d
            shift = d
        a_sh = jnp.where(keep, pltpu.roll(a, shift, 1), 1.0)
        u_sh = jnp.where(keep, pltpu.roll(u, shift, 1), 0.0)
        u = a * u_sh + u
        a = a * a_sh
    a_s[...] = a.reshape(TS, D_RNN)
    u_s[...] = u.reshape(TS, D_RNN)
    c = carry_s[...]
    for g in (range(n_groups - 1, -1, -1) if reverse else range(n_groups)):
        c_s[g:g + 1, :] = c
        e = g * SUBLANES + (0 if reverse else SUBLANES - 1)
        c = a_s[e:e + 1, :] * c + u_s[e:e + 1, :]
    carry_s[...] = c
    for g in range(n_groups):
        rows = slice(g * SUBLANES, (g + 1) * SUBLANES)
        o_ref[rows, :] = u_s[rows, :] + a_s[rows, :] * c_s[g:g + 1, :]


def _lru(xc, w, d, batch, seq, reverse):
    m = xc.shape[0]
    n_tiles = seq // TS
    main = pl.BlockSpec((TS, D_RNN), lambda b, i: (b * n_tiles + ((n_tiles - 1 - i) if reverse else i), 0))
    vec = _const_spec((1, D_RNN))
    mat = _const_spec((D_RNN, D_RNN))
    big = pltpu.VMEM((TS, D_RNN), F32)
    return pl.pallas_call(
        functools.partial(_lru_body, reverse=reverse), grid=(batch, n_tiles),
        in_specs=[main, mat, vec, mat, vec, vec],
        out_specs=main, out_shape=jax.ShapeDtypeStruct((m, D_RNN), F32),
        scratch_shapes=[big, big, pltpu.VMEM((TS // SUBLANES, D_RNN), F32), pltpu.VMEM((1, D_RNN), F32)],
        compiler_params=_params("arbitrary", "arbitrary"), name="lru_bwd" if reverse else "lru_fwd")(
            xc, w["wa"][d], w["ba"][d], w["wx"][d], w["bx"][d], w["lam"][d])


def _odd_out_body(sink_ref, q_ref, k_ref, kp_ref, kn_ref, v_ref, vp_ref, vn_ref, gate_ref, hf_ref, hb_ref,
                  wo_ref, x_ref, o_ref, kbuf, vbuf, s_buf, p_buf, yd_s, *, tiles_per_seq, seq):
    t = pl.program_id(0) % tiles_per_seq
    kbuf[0:BLOCK, :] = kp_ref[...]
    kbuf[BLOCK:BLOCK + TW, :] = k_ref[...]
    kbuf[BLOCK + TW:, :] = kn_ref[...]
    vbuf[0:BLOCK, :] = vp_ref[...]
    vbuf[BLOCK:BLOCK + TW, :] = v_ref[...]
    vbuf[BLOCK + TW:, :] = vn_ref[...]

    qi = lax.broadcasted_iota(jnp.int32, (BLOCK, 3 * BLOCK), 0)
    ci = lax.broadcasted_iota(jnp.int32, (BLOCK, 3 * BLOCK), 1)
    rel = jnp.abs(ci - BLOCK - qi)
    relf = rel.astype(F32)
    in_window = rel <= WINDOW
    low_half = lax.broadcasted_iota(jnp.int32, (BLOCK, LANES), 1) < SWA_HD

    units = [(jb, g) for jb in range(TW // BLOCK) for g in range(SWA_KV_HEADS)]

    def scores(u):
        jb, g = units[u]
        qs = jnp.concatenate([q_ref[jb * BLOCK:(jb + 1) * BLOCK, h * LANES:(h + 1) * LANES]
                              for h in range(g * SWA_GROUP, (g + 1) * SWA_GROUP)], axis=0)
        s_buf[u % 2] = _dot_nt(qs, kbuf[jb * BLOCK:(jb + 3) * BLOCK, g * LANES:(g + 1) * LANES])

    scores(0)
    for u, (jb, g) in enumerate(units):
        if u + 1 < len(units):
            scores(u + 1)
        key_pos = t * TW + (jb - 1) * BLOCK + ci
        valid = in_window & (key_pos >= 0) & (key_pos < seq)
        sink_terms = []
        for hh in range(SWA_GROUP):
            h = g * SWA_GROUP + hh
            rows = slice(hh * BLOCK, (hh + 1) * BLOCK)
            s = jnp.where(valid, s_buf[u % 2, rows, :] - (2.0 ** -(h + 1)) * relf, -jnp.inf)
            sk = sink_ref[h]
            mx = jnp.maximum(jnp.max(s, axis=-1, keepdims=True), sk)
            p_buf[u % 2, rows, :] = jnp.exp(s - mx).astype(BF16)
            sink_terms.append(jnp.exp(sk - mx))
        o = _dot(p_buf[u % 2], vbuf[jb * BLOCK:(jb + 3) * BLOCK, g * LANES:(g + 1) * LANES])
        for pp in range(SWA_GROUP // 2):
            oe = o[2 * pp * BLOCK:(2 * pp + 1) * BLOCK, :]
            oo = o[(2 * pp + 1) * BLOCK:(2 * pp + 2) * BLOCK, :]
            ye = oe / (pltpu.roll(oe, SWA_HD, 1) + sink_terms[2 * pp])
            yo = pltpu.roll(oo, SWA_HD, 1) / (oo + sink_terms[2 * pp + 1])
            pair = (g * SWA_GROUP) // 2 + pp
            yd_s[jb * BLOCK:(jb + 1) * BLOCK, pair * LANES:(pair + 1) * LANES] = (
                jnp.where(low_half, ye, yo).astype(BF16))

    yc = (jax.nn.gelu(gate_ref[...]) * (hf_ref[...] + hb_ref[...])).astype(BF16)
    y = _dot(yc, wo_ref[0:D_RNN, :]) + _dot(yd_s[...], wo_ref[D_RNN:, :])
    o_ref[...] = x_ref[...] + y


def _odd_out(sink, q, k, v, gate, hf, hb, wo, x, seq):
    m = x.shape[0]
    kw = SWA_KV_HEADS * LANES
    vw = SWA_KV_HEADS * LANES
    unit = (SWA_GROUP * BLOCK, 3 * BLOCK)
    per = TW // BLOCK
    last = m // BLOCK - 1
    row = lambda n: pl.BlockSpec((TW, n), lambda i: (i, 0))
    prev = lambda n: pl.BlockSpec((BLOCK, n), lambda i: (jnp.maximum(i * per - 1, 0), 0))
    nxt = lambda n: pl.BlockSpec((BLOCK, n), lambda i: (jnp.minimum((i + 1) * per, last), 0))
    return pl.pallas_call(
        functools.partial(_odd_out_body, tiles_per_seq=seq // TW, seq=seq), grid=(m // TW,),
        in_specs=[pl.BlockSpec(memory_space=pltpu.SMEM),
                  row(SWA_HEADS * LANES), row(kw), prev(kw), nxt(kw), row(vw), prev(vw), nxt(vw),
                  row(D_RNN), row(D_RNN), row(D_RNN), _const_spec(wo.shape), row(D_MODEL)],
        out_specs=row(D_MODEL), out_shape=jax.ShapeDtypeStruct((m, D_MODEL), F32),
        scratch_shapes=[pltpu.VMEM((TW + 2 * BLOCK, kw), BF16), pltpu.VMEM((TW + 2 * BLOCK, vw), BF16),
                        pltpu.VMEM((2,) + unit, F32), pltpu.VMEM((2,) + unit, BF16),
                        pltpu.VMEM((TW, SWA_HEADS * SWA_HD), BF16)],
        compiler_params=_params("parallel"), name="odd_out")(
            sink, q, k, k, k, v, v, v, gate, hf, hb, wo, x)


def _rotate_half_cols(w):
    half = w.shape[-1] // 2
    return jnp.concatenate([-w[..., half:], w[..., :half]], axis=-1)


def _even_weights(w_in, conv_w, q_norm, w_uq, kv_norm, w_ukv, w_out):
    zc = lambda n: jnp.zeros((D_MODEL, n), F32)
    kr = w_in[:, EV_KR:]
    win = jnp.concatenate([w_in[:, :EV_KR], zc(QK_NOPE), kr, zc(LANES - QK_NOPE - QK_ROPE),
                           zc(QK_NOPE), _rotate_half_cols(kr), zc(LANES - QK_NOPE - QK_ROPE)], axis=1)
    wq = w_uq.reshape(Q_RANK, MLA_HEADS, QK_NOPE + QK_ROPE)
    nope, pe = wq[..., :QK_NOPE], wq[..., QK_NOPE:]
    zq = lambda n: jnp.zeros((Q_RANK, MLA_HEADS, n), F32)
    wqa = jnp.concatenate([nope, pe, zq(LANES - QK_NOPE - QK_ROPE)], -1).reshape(Q_RANK, MLA_HEADS * LANES)
    wqb = jnp.concatenate([zq(QK_NOPE), _rotate_half_cols(pe), zq(LANES - QK_NOPE - QK_ROPE)], -1)
    wqb = wqb.reshape(Q_RANK, MLA_HEADS * LANES)
    wkv = w_ukv.reshape(KV_RANK, MLA_HEADS, QK_NOPE + V_DIM)
    wuk = jnp.concatenate([wkv[..., :QK_NOPE], jnp.zeros((KV_RANK, MLA_HEADS, LANES - QK_NOPE), F32)], -1)
    wuv = jnp.concatenate([wkv[..., QK_NOPE:], jnp.zeros((KV_RANK, MLA_HEADS, LANES - V_DIM), F32)], -1)
    return dict(win=win.astype(BF16), conv_w=conv_w, q_norm=q_norm[None, :], kv_norm=kv_norm[None, :],
                wqa=wqa.astype(BF16), wqb=wqb.astype(BF16),
                wuk=wuk.reshape(KV_RANK, MLA_HEADS * LANES).astype(BF16),
                wuv=wuv.reshape(KV_RANK, MLA_HEADS * LANES).astype(BF16), wo=w_out.astype(BF16))


def _block_diag(w):
    eye = jnp.eye(LRU_BLOCKS, dtype=w.dtype)
    return jnp.einsum('ncd,nm->ncmd', w, eye).reshape(D_RNN, D_RNN)


def _odd_weights(w_in, conv_w, conv_b, w_a, b_a, w_x, b_x, lam, sink, w_out):
    qd = SWA_HEADS * SWA_HD
    kd = SWA_KV_HEADS * SWA_HD
    pad = lambda w, heads: jnp.concatenate(
        [w.reshape(D_MODEL, heads, SWA_HD), jnp.zeros((D_MODEL, heads, LANES - SWA_HD), F32)], -1
    ).reshape(D_MODEL, heads * LANES)
    wq = w_in[:, OD_Q:OD_Q + qd]
    wk = w_in[:, OD_Q + qd:OD_Q + qd + kd]
    wv = w_in[:, OD_Q + qd + kd:]
    win = jnp.concatenate([w_in[:, :OD_Q], pad(wq, SWA_HEADS), pad(wk, SWA_KV_HEADS), pad(wv, SWA_KV_HEADS)],
                          axis=1)
    return dict(win=win.astype(BF16), conv_w=conv_w, conv_b=conv_b[None, :],
                wa=[_block_diag(w_a[d]).astype(BF16) for d in range(2)], ba=[b_a[d][None, :] for d in range(2)],
                wx=[_block_diag(w_x[d]).astype(BF16) for d in range(2)], bx=[b_x[d][None, :] for d in range(2)],
                lam=[lam[d][None, :] for d in range(2)], sink=sink, wo=w_out.astype(BF16))


def _rope_tables(seq):
    half = QK_ROPE // 2
    inv = ROPE_THETA ** (-jnp.arange(half, dtype=F32) / half)
    ang = jnp.arange(seq).astype(F32)[:, None] * inv[None, :]
    z_lo = jnp.zeros((seq, QK_NOPE), F32)
    z_hi = jnp.zeros((seq, LANES - QK_NOPE - QK_ROPE), F32)
    cos = jnp.concatenate([z_lo, jnp.cos(ang), jnp.cos(ang), z_hi], 1)
    sin = jnp.concatenate([z_lo, jnp.sin(ang), jnp.sin(ang), z_hi], 1)
    return jnp.concatenate([cos, sin], 1)


def _trunk(x3, tab, ffn, mix_norm, even, odd, final_norm):
    batch, seq, _ = x3.shape
    x = x3.reshape(batch * seq, D_MODEL)
    depth = len(ffn)
    for l in range(depth):
        x = _ffn(x, *ffn[l][0])
        g = mix_norm[l][None, :]
        if l % 2 == 0:
            w = even[l // 2]
            bg, u, q, k, v = _even_in(x, g, w, tab, seq)
            yb = _flash(q, k, v, batch, seq)
            x = _even_out(bg, u, w["conv_w"], yb, w["wo"], x, seq)
        else:
            w = odd[l // 2]
            xc, gate, q, k, v = _odd_in(x, g, w, seq)
            hf = _lru(xc, w, 0, batch, seq, reverse=False)
            hb = _lru(xc, w, 1, batch, seq, reverse=True)
            x = _odd_out(w["sink"], q, k, v, gate, hf, hb, w["wo"], x, seq)
        x = _ffn(x, *ffn[l][1], final_g=final_norm[None, :] if l == depth - 1 else None)
    return x.reshape(batch, seq, D_MODEL)


def kernel(x_prompt, x_sample, ffn_norm, ffn_w_gate, ffn_w_up, ffn_w_down, mix_norm, ev_w_in, ev_conv_w, mla_q_norm, mla_w_uq, mla_kv_norm, mla_w_ukv, ev_w_out, od_w_in, od_conv_w, od_conv_b, lru_w_a, lru_b_a, lru_w_x, lru_b_x, lru_lambda, swa_sink, od_w_out, final_norm):
    depth = ffn_norm.shape[0]
    ffn = [[(ffn_norm[l, s][None, :], ffn_w_gate[l, s].astype(BF16), ffn_w_up[l, s].astype(BF16),
             ffn_w_down[l, s].astype(BF16)) for s in range(2)] for l in range(depth)]
    even = [_even_weights(ev_w_in[j], ev_conv_w[j], mla_q_norm[j], mla_w_uq[j], mla_kv_norm[j], mla_w_ukv[j],
                          ev_w_out[j]) for j in range(ev_w_in.shape[0])]
    odd = [_odd_weights(od_w_in[j], od_conv_w[j], od_conv_b[j], lru_w_a[j], lru_b_a[j], lru_w_x[j], lru_b_x[j],
                        lru_lambda[j], swa_sink[j], od_w_out[j]) for j in range(od_w_in.shape[0])]
    tab = _rope_tables(max(x_prompt.shape[1], x_sample.shape[1]))
    y_prompt = _trunk(x_prompt, tab, ffn, mix_norm, even, odd, final_norm)
    y_sample = _trunk(x_sample, tab, ffn, mix_norm, even, odd, final_norm)
    return (y_prompt, y_sample)
```

```python
import functools
import math

import jax
import jax.numpy as jnp
from jax import lax
from jax.experimental import pallas as pl
from jax.experimental.pallas import tpu as pltpu

F32 = jnp.float32
BF16 = jnp.bfloat16

EPS = 1e-6
D_MODEL = 1024
D_FF = 2816
BLOCK = 128
LANES = 128
SUBLANES = 8

D_CONV = 512
MLA_HEADS = 8
QK_NOPE = 64
QK_ROPE = 32
V_DIM = 64
Q_RANK = 384
KV_RANK = 256
ROPE_THETA = 10000.0

D_RNN = 512
LRU_BLOCKS = 8
LRU_BW = D_RNN // LRU_BLOCKS
LRU_C = 8.0

SWA_HEADS = 8
SWA_KV_HEADS = 2
SWA_HD = 64
SWA_GROUP = SWA_HEADS // SWA_KV_HEADS
WINDOW = 128

TM = 512
MXU_TILE = 256
FF_SPLITS = (0, 4 * MXU_TILE, 8 * MXU_TILE, D_FF)
TM_FFN = 1024
TQ = 1024
FLASH_UNIT = 512
TK = 2048
FLASH_KEYS = 1024
FLASH_AHEAD = 2
FLASH_SLOTS = 4
FLASH_ROWS = 32
TS = 512
TW = 512
HALO = SUBLANES

VMEM_LIMIT = 56 * 1024 * 1024
FLASH_VMEM_LIMIT = VMEM_LIMIT

EV_QLAT = 3 * D_CONV
EV_KVLAT = EV_QLAT + Q_RANK
EV_KR = EV_KVLAT + KV_RANK
EV_IN_EXT = EV_KR + 2 * LANES
OD_Q = 2 * D_RNN
OD_K = OD_Q + SWA_HEADS * LANES
OD_V = OD_K + SWA_KV_HEADS * LANES
OD_IN_EXT = OD_V + SWA_KV_HEADS * LANES


def _const_spec(shape):
    zeros = (0,) * len(shape)
    return pl.BlockSpec(shape, lambda *_: zeros, pipeline_mode=pl.Buffered(1))


def _params(*sem):
    return pltpu.CompilerParams(dimension_semantics=sem, vmem_limit_bytes=VMEM_LIMIT)


def _rms(x, g):
    ms = jnp.mean(x * x, axis=-1, keepdims=True)
    return x * lax.rsqrt(ms + EPS) * g


def _dot(a, b):
    return jnp.dot(a, b, preferred_element_type=F32)


def _dot_nt(a, b):
    return lax.dot_general(a, b, (((1,), (1,)), ((), ())), preferred_element_type=F32)


def _swiglu_residual(x, g_ref, wg_ref, wu_ref, wd_ref):
    xn = _rms(x, g_ref[...]).astype(BF16)
    acc = None
    for lo, hi in zip(FF_SPLITS[:-1], FF_SPLITS[1:]):
        sl = slice(lo, hi)
        gate = _dot(xn, wg_ref[:, sl])
        up = _dot(xn, wu_ref[:, sl])
        h = (gate * jax.nn.sigmoid(gate) * up).astype(BF16)
        y = _dot(h, wd_ref[sl, :])
        acc = y if acc is None else acc + y
    return x + 0.5 * acc


def _ffn_body(x_ref, g_ref, wg_ref, wu_ref, wd_ref, *rest, final):
    o_ref = rest[-1]
    out = _swiglu_residual(x_ref[...], g_ref, wg_ref, wu_ref, wd_ref)
    if final:
        out = _rms(out, rest[0][...])
    o_ref[...] = out


def _ffn(x, g, wg, wu, wd, final_g=None):
    m = x.shape[0]
    row = pl.BlockSpec((TM_FFN, D_MODEL), lambda i: (i, 0))
    in_specs = [row, _const_spec((1, D_MODEL)), _const_spec((D_MODEL, D_FF)),
                _const_spec((D_MODEL, D_FF)), _const_spec((D_FF, D_MODEL))]
    args = [x, g, wg, wu, wd]
    if final_g is not None:
        in_specs.append(_const_spec((1, D_MODEL)))
        args.append(final_g)
    return pl.pallas_call(
        functools.partial(_ffn_body, final=final_g is not None),
        grid=(m // TM_FFN,), in_specs=in_specs, out_specs=row,
        out_shape=jax.ShapeDtypeStruct((m, D_MODEL), F32),
        compiler_params=_params("parallel"), name="ffn")(*args)


def _even_in_body(x_ref, g_ref, win_ref, qg_ref, kvg_ref, wqa_ref, wqb_ref, wuk_ref, wuv_ref, tab_ref,
                  bg_ref, u_ref, q_ref, k_ref, v_ref):
    xn = _rms(x_ref[...], g_ref[...]).astype(BF16)
    z = _dot(xn, win_ref[...])
    bg_ref[...] = z[:, :D_CONV]
    u_ref[...] = z[:, D_CONV:2 * D_CONV] * z[:, 2 * D_CONV:3 * D_CONV]
    qn = _rms(z[:, EV_QLAT:EV_KVLAT], qg_ref[...]).astype(BF16)
    kvn = _rms(z[:, EV_KVLAT:EV_KR], kvg_ref[...]).astype(BF16)
    cosk = tab_ref[:, 0:LANES]
    sink = tab_ref[:, LANES:2 * LANES]
    scale = (QK_NOPE + QK_ROPE) ** -0.5 * math.log2(math.e)
    nope = lax.broadcasted_iota(jnp.int32, (1, LANES), 1) < QK_NOPE
    cosq = scale * jnp.where(nope, 1.0, cosk)
    sinq = scale * sink
    kpe = z[:, EV_KR:EV_KR + LANES] * cosk + z[:, EV_KR + LANES:EV_KR + 2 * LANES] * sink
    qa = _dot(qn, wqa_ref[...])
    qb = _dot(qn, wqb_ref[...])
    kn = _dot(kvn, wuk_ref[...])
    lane = lax.broadcasted_iota(jnp.int32, (1, MLA_HEADS * LANES), 1)
    ones_lanes = jnp.where(lane % LANES >= V_DIM, 1.0, 0.0)
    v_ref[...] = (_dot(kvn, wuv_ref[...]) + ones_lanes).astype(BF16)
    for h in range(MLA_HEADS):
        sl = slice(h * LANES, (h + 1) * LANES)
        q_ref[:, sl] = (qa[:, sl] * cosq + qb[:, sl] * sinq).astype(BF16)
        k_ref[:, sl] = (kn[:, sl] + kpe).astype(BF16)


def _even_in(x, g, w, tab, seq):
    m = x.shape[0]
    hw = MLA_HEADS * LANES
    tiles_per_seq = seq // TM
    row = lambda n: pl.BlockSpec((TM, n), lambda i: (i, 0))
    in_specs = [row(D_MODEL), _const_spec((1, D_MODEL)), _const_spec((D_MODEL, EV_IN_EXT)),
                _const_spec((1, Q_RANK)), _const_spec((1, KV_RANK)),
                _const_spec((Q_RANK, hw)), _const_spec((Q_RANK, hw)),
                _const_spec((KV_RANK, hw)), _const_spec((KV_RANK, hw)),
                pl.BlockSpec((TM, 2 * LANES), lambda i: (i % tiles_per_seq, 0))]
    out_specs = [row(D_CONV), row(D_CONV), row(hw), row(hw), row(hw)]
    out_shape = [jax.ShapeDtypeStruct((m, D_CONV), F32), jax.ShapeDtypeStruct((m, D_CONV), F32),
                 jax.ShapeDtypeStruct((m, hw), BF16), jax.ShapeDtypeStruct((m, hw), BF16),
                 jax.ShapeDtypeStruct((m, hw), BF16)]
    return pl.pallas_call(
        _even_in_body, grid=(m // TM,), in_specs=in_specs, out_specs=out_specs, out_shape=out_shape,
        compiler_params=_params("parallel"), name="even_in")(
            x, g, w["win"], w["q_norm"], w["kv_norm"], w["wqa"], w["wqb"], w["wuk"], w["wuv"], tab)


def _flash_body(q_ref, k_ref, v_ref, o_ref, m_ref, acc_ref, s_buf, p_buf, *, n_kv):
    j = pl.program_id(2)

    @pl.when(j == 0)
    def _():
        m_ref[...] = jnp.full(m_ref.shape, -jnp.inf, F32)
        acc_ref[...] = jnp.zeros(acc_ref.shape, F32)

    units = [(h, r0) for h in range(MLA_HEADS) for r0 in range(0, TQ, FLASH_UNIT)]
    n_chunks = TK // FLASH_KEYS
    assert len(units) % FLASH_SLOTS == 0

    def scores(u, k0):
        h, r0 = units[u]
        sl = slice(h * LANES, (h + 1) * LANES)
        s_buf[u % FLASH_SLOTS] = _dot_nt(q_ref[r0:r0 + FLASH_UNIT, sl], k_ref[pl.ds(k0, FLASH_KEYS), sl])

    def chunk(c, carry):
        k0 = pl.multiple_of(c * FLASH_KEYS, FLASH_KEYS)
        k_next = pl.multiple_of(jnp.minimum(c + 1, n_chunks - 1) * FLASH_KEYS, FLASH_KEYS)
        for u, (h, r0) in enumerate(units):
            ahead = u + FLASH_AHEAD
            if ahead < len(units):
                scores(ahead, k0)
            else:
                scores(ahead - len(units), k_next)
            for r in range(0, FLASH_UNIT, FLASH_ROWS):
                rows = slice(r0 + r, r0 + r + FLASH_ROWS)
                s = s_buf[u % FLASH_SLOTS, r:r + FLASH_ROWS, :]
                m_prev = m_ref[h, rows, :]
                m_new = jnp.maximum(m_prev, jnp.max(s, axis=-1, keepdims=True))
                p_buf[u % 2, r:r + FLASH_ROWS, :] = jnp.exp2(s - m_new).astype(BF16)
                acc_ref[h, rows, :] = acc_ref[h, rows, :] * jnp.exp2(m_prev - m_new)
                m_ref[h, rows, :] = m_new
            acc_ref[h, r0:r0 + FLASH_UNIT, :] += _dot(p_buf[u % 2],
                                                      v_ref[pl.ds(k0, FLASH_KEYS), h * LANES:(h + 1) * LANES])
        return carry

    for u in range(FLASH_AHEAD):
        scores(u, 0)
    lax.fori_loop(0, n_chunks, chunk, 0)

    @pl.when(j == n_kv - 1)
    def _():
        low_half = lax.broadcasted_iota(jnp.int32, (TQ, LANES), 1) < V_DIM
        for pair in range(MLA_HEADS // 2):
            even = acc_ref[2 * pair]
            odd = acc_ref[2 * pair + 1]
            even_sw = pltpu.roll(even, V_DIM, 1)
            odd_sw = pltpu.roll(odd, V_DIM, 1)
            out = jnp.where(low_half, even / even_sw, odd_sw / odd)
            o_ref[:, pair * LANES:(pair + 1) * LANES] = out.astype(BF16)


def _flash(q, k, v, batch, seq):
    m = q.shape[0]
    hw = MLA_HEADS * LANES
    nq, nk = seq // TQ, seq // TK
    ow = MLA_HEADS * V_DIM
    return pl.pallas_call(
        functools.partial(_flash_body, n_kv=nk), grid=(batch, nq, nk),
        in_specs=[pl.BlockSpec((TQ, hw), lambda b, i, j: (b * nq + i, 0)),
                  pl.BlockSpec((TK, hw), lambda b, i, j: (b * nk + j, 0)),
                  pl.BlockSpec((TK, hw), lambda b, i, j: (b * nk + j, 0))],
        out_specs=pl.BlockSpec((TQ, ow), lambda b, i, j: (b * nq + i, 0)),
        out_shape=jax.ShapeDtypeStruct((m, ow), BF16),
        scratch_shapes=[pltpu.VMEM((MLA_HEADS, TQ, 1), F32), pltpu.VMEM((MLA_HEADS, TQ, LANES), F32),
                        pltpu.VMEM((FLASH_SLOTS, FLASH_UNIT, FLASH_KEYS), F32),
                        pltpu.VMEM((2, FLASH_UNIT, FLASH_KEYS), BF16)],
        compiler_params=pltpu.CompilerParams(dimension_semantics=("parallel", "parallel", "arbitrary"),
                                             vmem_limit_bytes=FLASH_VMEM_LIMIT), name="mla_flash")(q, k, v)


def _even_out_ffn_body(bg_ref, u_ref, up_ref, un_ref, cw_ref, yb_ref, wo_ref, x_ref, g_ref, wg_ref, wu_ref, wd_ref,
                       o_ref, *, tiles_per_seq):
    t = pl.program_id(0) % tiles_per_seq
    u = u_ref[...]
    row = lax.broadcasted_iota(jnp.int32, u.shape, 0)
    prev = jnp.where(t == 0, 0.0, up_ref[HALO - 1:HALO, :])
    nxt = jnp.where(t == tiles_per_seq - 1, 0.0, un_ref[0:1, :])
    um1 = jnp.where(row == 0, prev, pltpu.roll(u, 1, 0))
    up1 = jnp.where(row == TM - 1, nxt, pltpu.roll(u, TM - 1, 0))
    conv = um1 * cw_ref[0:1, :] + u * cw_ref[1:2, :] + up1 * cw_ref[2:3, :]
    ya = (bg_ref[...] * conv).astype(BF16)
    x = x_ref[...] + _dot(ya, wo_ref[0:D_CONV, :]) + _dot(yb_ref[...], wo_ref[D_CONV:, :])
    o_ref[...] = _swiglu_residual(x, g_ref, wg_ref, wu_ref, wd_ref)


def _halo_specs(tile, width, n_rows):
    per = tile // HALO
    last = n_rows // HALO - 1
    prev = pl.BlockSpec((HALO, width), lambda i: (jnp.maximum(i * per - 1, 0), 0))
    nxt = pl.BlockSpec((HALO, width), lambda i: (jnp.minimum((i + 1) * per, last), 0))
    return prev, nxt


def _even_out_ffn(bg, u, conv_w, yb, wo, x, g, wg, wu, wd, seq):
    m = x.shape[0]
    row = lambda n: pl.BlockSpec((TM, n), lambda i: (i, 0))
    prev, nxt = _halo_specs(TM, D_CONV, m)
    return pl.pallas_call(
        functools.partial(_even_out_ffn_body, tiles_per_seq=seq // TM), grid=(m // TM,),
        in_specs=[row(D_CONV), row(D_CONV), prev, nxt, _const_spec(conv_w.shape), row(MLA_HEADS * V_DIM),
                  _const_spec(wo.shape), row(D_MODEL), _const_spec((1, D_MODEL)), _const_spec((D_MODEL, D_FF)),
                  _const_spec((D_MODEL, D_FF)), _const_spec((D_FF, D_MODEL))],
        out_specs=row(D_MODEL), out_shape=jax.ShapeDtypeStruct((m, D_MODEL), F32),
        compiler_params=_params("parallel"), name="even_out_ffn")(bg, u, u, u, conv_w, yb, wo, x, g, wg, wu, wd)


def _odd_in_body(x_ref, xp_ref, xn_ref, g_ref, win_ref, cw_ref, cb_ref, xc_ref, gate_ref, q_ref, k_ref, v_ref, *,
                 tiles_per_seq):
    t = pl.program_id(0) % tiles_per_seq
    x_ext = jnp.concatenate([xp_ref[...], x_ref[...], xn_ref[...]], axis=0)
    z_ext = _dot(_rms(x_ext, g_ref[...]).astype(BF16), win_ref[...])
    z = z_ext[HALO:HALO + TM, :]
    xr = z_ext[:, :D_RNN]
    row = lax.broadcasted_iota(jnp.int32, xr.shape, 0)
    outside = ((row < HALO) & (t == 0)) | ((row >= HALO + TM) & (t == tiles_per_seq - 1))
    xr = jnp.where(outside, 0.0, xr)
    n_ext = TM + 2 * HALO
    xc = (pltpu.roll(xr, 2, 0) * cw_ref[0:1, :] + pltpu.roll(xr, 1, 0) * cw_ref[1:2, :] + xr * cw_ref[2:3, :]
          + pltpu.roll(xr, n_ext - 1, 0) * cw_ref[3:4, :])
    xc_ref[...] = xc[HALO:HALO + TM, :] + cb_ref[...]
    gate_ref[...] = z[:, D_RNN:OD_Q]
    q_ref[...] = (z[:, OD_Q:OD_K] * (SWA_HD ** -0.5)).astype(BF16)
    k_ref[...] = z[:, OD_K:OD_V].astype(BF16)
    lane = lax.broadcasted_iota(jnp.int32, (1, SWA_KV_HEADS * LANES), 1)
    v_ref[...] = (z[:, OD_V:] + jnp.where(lane % LANES >= SWA_HD, 1.0, 0.0)).astype(BF16)


def _odd_in(x, g, w, seq):
    m = x.shape[0]
    row = lambda n: pl.BlockSpec((TM, n), lambda i: (i, 0))
    prev, nxt = _halo_specs(TM, D_MODEL, m)
    widths = [D_RNN, D_RNN, SWA_HEADS * LANES, SWA_KV_HEADS * LANES, SWA_KV_HEADS * LANES]
    dtypes = [F32, F32, BF16, BF16, BF16]
    return pl.pallas_call(
        functools.partial(_odd_in_body, tiles_per_seq=seq // TM), grid=(m // TM,),
        in_specs=[row(D_MODEL), prev, nxt, _const_spec((1, D_MODEL)), _const_spec((D_MODEL, OD_IN_EXT)),
                  _const_spec(w["conv_w"].shape), _const_spec((1, D_RNN))],
        out_specs=[row(n) for n in widths],
        out_shape=[jax.ShapeDtypeStruct((m, n), d) for n, d in zip(widths, dtypes)],
        compiler_params=_params("parallel"), name="odd_in")(x, x, x, g, w["win"], w["conv_w"], w["conv_b"])


def _lru_body(x_ref, wa_ref, ba_ref, wx_ref, bx_ref, lam_ref, o_ref, a_s, u_s, c_s, carry_s, *, reverse):
    @pl.when(pl.program_id(1) == 0)
    def _():
        carry_s[...] = jnp.zeros(carry_s.shape, F32)

    xc = x_ref[...]
    xcb = xc.astype(BF16)
    tr = jnp.tanh(0.5 * (_dot(xcb, wa_ref[...]) + ba_ref[...]))
    ti = jnp.tanh(0.5 * (_dot(xcb, wx_ref[...]) + bx_ref[...]))
    nl = -lam_ref[...]
    softplus = jnp.maximum(nl, 0.0) + jnp.log1p(jnp.exp(-jnp.abs(nl)))
    log_a = (-0.5 * LRU_C * softplus) * (tr + 1.0)
    a = jnp.exp(log_a)
    th = jnp.tanh(log_a)
    u = jnp.sqrt(-0.5 * th / (1.0 - th)) * ((ti + 1.0) * xc)
    n_groups = TS // SUBLANES
    a = a.reshape(n_groups, SUBLANES, D_RNN)
    u = u.reshape(n_groups, SUBLANES, D_RNN)
    in_group = lax.broadcasted_iota(jnp.int32, a.shape, 1)
    for d in (1, 2, 4):
        if reverse:
            keep = in_group < SUBLANES - d
            shift = SUBLANES - d
        else:
            keep = in_group >= d
            shift = d
        a_sh = jnp.where(keep, pltpu.roll(a, shift, 1), 1.0)
        u_sh = jnp.where(keep, pltpu.roll(u, shift, 1), 0.0)
        u = a * u_sh + u
        a = a * a_sh
    a_s[...] = a.reshape(TS, D_RNN)
    u_s[...] = u.reshape(TS, D_RNN)
    c = carry_s[...]
    for g in (range(n_groups - 1, -1, -1) if reverse else range(n_groups)):
        c_s[g:g + 1, :] = c
        e = g * SUBLANES + (0 if reverse else SUBLANES - 1)
        c = a_s[e:e + 1, :] * c + u_s[e:e + 1, :]
    carry_s[...] = c
    for g in range(n_groups):
        rows = slice(g * SUBLANES, (g + 1) * SUBLANES)
        o_ref[rows, :] = u_s[rows, :] + a_s[rows, :] * c_s[g:g + 1, :]


def _lru(xc, w, d, batch, seq, reverse):
    m = xc.shape[0]
    n_tiles = seq // TS
    main = pl.BlockSpec((TS, D_RNN), lambda b, i: (b * n_tiles + ((n_tiles - 1 - i) if reverse else i), 0))
    vec = _const_spec((1, D_RNN))
    mat = _const_spec((D_RNN, D_RNN))
    big = pltpu.VMEM((TS, D_RNN), F32)
    return pl.pallas_call(
        functools.partial(_lru_body, reverse=reverse), grid=(batch, n_tiles),
        in_specs=[main, mat, vec, mat, vec, vec],
        out_specs=main, out_shape=jax.ShapeDtypeStruct((m, D_RNN), F32),
        scratch_shapes=[big, big, pltpu.VMEM((TS // SUBLANES, D_RNN), F32), pltpu.VMEM((1, D_RNN), F32)],
        compiler_params=_params("arbitrary", "arbitrary"), name="lru_bwd" if reverse else "lru_fwd")(
            xc, w["wa"][d], w["ba"][d], w["wx"][d], w["bx"][d], w["lam"][d])


def _odd_out_body(sink_ref, q_ref, k_ref, kp_ref, kn_ref, v_ref, vp_ref, vn_ref, gate_ref, hf_ref, hb_ref,
                  wo_ref, x_ref, o_ref, kbuf, vbuf, s_buf, p_buf, yd_s, *, tiles_per_seq, seq):
    t = pl.program_id(0) % tiles_per_seq
    kbuf[0:BLOCK, :] = kp_ref[...]
    kbuf[BLOCK:BLOCK + TW, :] = k_ref[...]
    kbuf[BLOCK + TW:, :] = kn_ref[...]
    vbuf[0:BLOCK, :] = vp_ref[...]
    vbuf[BLOCK:BLOCK + TW, :] = v_ref[...]
    vbuf[BLOCK + TW:, :] = vn_ref[...]

    qi = lax.broadcasted_iota(jnp.int32, (BLOCK, 3 * BLOCK), 0)
    ci = lax.broadcasted_iota(jnp.int32, (BLOCK, 3 * BLOCK), 1)
    rel = jnp.abs(ci - BLOCK - qi)
    relf = rel.astype(F32)
    in_window = rel <= WINDOW
    low_half = lax.broadcasted_iota(jnp.int32, (BLOCK, LANES), 1) < SWA_HD

    units = [(jb, g) for jb in range(TW // BLOCK) for g in range(SWA_KV_HEADS)]

    def scores(u):
        jb, g = units[u]
        qs = jnp.concatenate([q_ref[jb * BLOCK:(jb + 1) * BLOCK, h * LANES:(h + 1) * LANES]
                              for h in range(g * SWA_GROUP, (g + 1) * SWA_GROUP)], axis=0)
        s_buf[u % 2] = _dot_nt(qs, kbuf[jb * BLOCK:(jb + 3) * BLOCK, g * LANES:(g + 1) * LANES])

    scores(0)
    for u, (jb, g) in enumerate(units):
        if u + 1 < len(units):
            scores(u + 1)
        key_pos = t * TW + (jb - 1) * BLOCK + ci
        valid = in_window & (key_pos >= 0) & (key_pos < seq)
        sink_terms = []
        for hh in range(SWA_GROUP):
            h = g * SWA_GROUP + hh
            rows = slice(hh * BLOCK, (hh + 1) * BLOCK)
            s = jnp.where(valid, s_buf[u % 2, rows, :] - (2.0 ** -(h + 1)) * relf, -jnp.inf)
            sk = sink_ref[h]
            mx = jnp.maximum(jnp.max(s, axis=-1, keepdims=True), sk)
            p_buf[u % 2, rows, :] = jnp.exp(s - mx).astype(BF16)
            sink_terms.append(jnp.exp(sk - mx))
        o = _dot(p_buf[u % 2], vbuf[jb * BLOCK:(jb + 3) * BLOCK, g * LANES:(g + 1) * LANES])
        for pp in range(SWA_GROUP // 2):
            oe = o[2 * pp * BLOCK:(2 * pp + 1) * BLOCK, :]
            oo = o[(2 * pp + 1) * BLOCK:(2 * pp + 2) * BLOCK, :]
            ye = oe / (pltpu.roll(oe, SWA_HD, 1) + sink_terms[2 * pp])
            yo = pltpu.roll(oo, SWA_HD, 1) / (oo + sink_terms[2 * pp + 1])
            pair = (g * SWA_GROUP) // 2 + pp
            yd_s[jb * BLOCK:(jb + 1) * BLOCK, pair * LANES:(pair + 1) * LANES] = (
                jnp.where(low_half, ye, yo).astype(BF16))

    yc = (jax.nn.gelu(gate_ref[...]) * (hf_ref[...] + hb_ref[...])).astype(BF16)
    y = _dot(yc, wo_ref[0:D_RNN, :]) + _dot(yd_s[...], wo_ref[D_RNN:, :])
    o_ref[...] = x_ref[...] + y


def _odd_out(sink, q, k, v, gate, hf, hb, wo, x, seq):
    m = x.shape[0]
    kw = SWA_KV_HEADS * LANES
    vw = SWA_KV_HEADS * LANES
    unit = (SWA_GROUP * BLOCK, 3 * BLOCK)
    per = TW // BLOCK
    last = m // BLOCK - 1
    row = lambda n: pl.BlockSpec((TW, n), lambda i: (i, 0))
    prev = lambda n: pl.BlockSpec((BLOCK, n), lambda i: (jnp.maximum(i * per - 1, 0), 0))
    nxt = lambda n: pl.BlockSpec((BLOCK, n), lambda i: (jnp.minimum((i + 1) * per, last), 0))
    return pl.pallas_call(
        functools.partial(_odd_out_body, tiles_per_seq=seq // TW, seq=seq), grid=(m // TW,),
        in_specs=[pl.BlockSpec(memory_space=pltpu.SMEM),
                  row(SWA_HEADS * LANES), row(kw), prev(kw), nxt(kw), row(vw), prev(vw), nxt(vw),
                  row(D_RNN), row(D_RNN), row(D_RNN), _const_spec(wo.shape), row(D_MODEL)],
        out_specs=row(D_MODEL), out_shape=jax.ShapeDtypeStruct((m, D_MODEL), F32),
        scratch_shapes=[pltpu.VMEM((TW + 2 * BLOCK, kw), BF16), pltpu.VMEM((TW + 2 * BLOCK, vw), BF16),
                        pltpu.VMEM((2,) + unit, F32), pltpu.VMEM((2,) + unit, BF16),
                        pltpu.VMEM((TW, SWA_HEADS * SWA_HD), BF16)],
        compiler_params=_params("parallel"), name="odd_out")(
            sink, q, k, k, k, v, v, v, gate, hf, hb, wo, x)


def _rotate_half_cols(w):
    half = w.shape[-1] // 2
    return jnp.concatenate([-w[..., half:], w[..., :half]], axis=-1)


def _even_weights(w_in, conv_w, q_norm, w_uq, kv_norm, w_ukv, w_out):
    zc = lambda n: jnp.zeros((D_MODEL, n), F32)
    kr = w_in[:, EV_KR:]
    win = jnp.concatenate([w_in[:, :EV_KR], zc(QK_NOPE), kr, zc(LANES - QK_NOPE - QK_ROPE),
                           zc(QK_NOPE), _rotate_half_cols(kr), zc(LANES - QK_NOPE - QK_ROPE)], axis=1)
    wq = w_uq.reshape(Q_RANK, MLA_HEADS, QK_NOPE + QK_ROPE)
    nope, pe = wq[..., :QK_NOPE], wq[..., QK_NOPE:]
    zq = lambda n: jnp.zeros((Q_RANK, MLA_HEADS, n), F32)
    wqa = jnp.concatenate([nope, pe, zq(LANES - QK_NOPE - QK_ROPE)], -1).reshape(Q_RANK, MLA_HEADS * LANES)
    wqb = jnp.concatenate([zq(QK_NOPE), _rotate_half_cols(pe), zq(LANES - QK_NOPE - QK_ROPE)], -1)
    wqb = wqb.reshape(Q_RANK, MLA_HEADS * LANES)
    wkv = w_ukv.reshape(KV_RANK, MLA_HEADS, QK_NOPE + V_DIM)
    wuk = jnp.concatenate([wkv[..., :QK_NOPE], jnp.zeros((KV_RANK, MLA_HEADS, LANES - QK_NOPE), F32)], -1)
    wuv = jnp.concatenate([wkv[..., QK_NOPE:], jnp.zeros((KV_RANK, MLA_HEADS, LANES - V_DIM), F32)], -1)
    return dict(win=win.astype(BF16), conv_w=conv_w, q_norm=q_norm[None, :], kv_norm=kv_norm[None, :],
                wqa=wqa.astype(BF16), wqb=wqb.astype(BF16),
                wuk=wuk.reshape(KV_RANK, MLA_HEADS * LANES).astype(BF16),
                wuv=wuv.reshape(KV_RANK, MLA_HEADS * LANES).astype(BF16), wo=w_out.astype(BF16))


def _block_diag(w):
    eye = jnp.eye(LRU_BLOCKS, dtype=w.dtype)
    return jnp.einsum('ncd,nm->ncmd', w, eye).reshape(D_RNN, D_RNN)


def _odd_weights(w_in, conv_w, conv_b, w_a, b_a, w_x, b_x, lam, sink, w_out):
    qd = SWA_HEADS * SWA_HD
    kd = SWA_KV_HEADS * SWA_HD
    pad = lambda w, heads: jnp.concatenate(
        [w.reshape(D_MODEL, heads, SWA_HD), jnp.zeros((D_MODEL, heads, LANES - SWA_HD), F32)], -1
    ).reshape(D_MODEL, heads * LANES)
    wq = w_in[:, OD_Q:OD_Q + qd]
    wk = w_in[:, OD_Q + qd:OD_Q + qd + kd]
    wv = w_in[:, OD_Q + qd + kd:]
    win = jnp.concatenate([w_in[:, :OD_Q], pad(wq, SWA_HEADS), pad(wk, SWA_KV_HEADS), pad(wv, SWA_KV_HEADS)],
                          axis=1)
    return dict(win=win.astype(BF16), conv_w=conv_w, conv_b=conv_b[None, :],
                wa=[_block_diag(w_a[d]).astype(BF16) for d in range(2)], ba=[b_a[d][None, :] for d in range(2)],
                wx=[_block_diag(w_x[d]).astype(BF16) for d in range(2)], bx=[b_x[d][None, :] for d in range(2)],
                lam=[lam[d][None, :] for d in range(2)], sink=sink, wo=w_out.astype(BF16))


def _rope_tables(seq):
    half = QK_ROPE // 2
    inv = ROPE_THETA ** (-jnp.arange(half, dtype=F32) / half)
    ang = jnp.arange(seq).astype(F32)[:, None] * inv[None, :]
    z_lo = jnp.zeros((seq, QK_NOPE), F32)
    z_hi = jnp.zeros((seq, LANES - QK_NOPE - QK_ROPE), F32)
    cos = jnp.concatenate([z_lo, jnp.cos(ang), jnp.cos(ang), z_hi], 1)
    sin = jnp.concatenate([z_lo, jnp.sin(ang), jnp.sin(ang), z_hi], 1)
    return jnp.concatenate([cos, sin], 1)


def _trunk(x3, tab, ffn, mix_norm, even, odd, final_norm):
    batch, seq, _ = x3.shape
    x = x3.reshape(batch * seq, D_MODEL)
    depth = len(ffn)
    for l in range(depth):
        x = _ffn(x, *ffn[l][0])
        g = mix_norm[l][None, :]
        if l % 2 == 0:
            w = even[l // 2]
            bg, u, q, k, v = _even_in(x, g, w, tab, seq)
            yb = _flash(q, k, v, batch, seq)
            assert l != depth - 1
            x = _even_out_ffn(bg, u, w["conv_w"], yb, w["wo"], x, *ffn[l][1], seq)
        else:
            w = odd[l // 2]
            xc, gate, q, k, v = _odd_in(x, g, w, seq)
            hf = _lru(xc, w, 0, batch, seq, reverse=False)
            hb = _lru(xc, w, 1, batch, seq, reverse=True)
            x = _odd_out(w["sink"], q, k, v, gate, hf, hb, w["wo"], x, seq)
            x = _ffn(x, *ffn[l][1], final_g=final_norm[None, :] if l == depth - 1 else None)
    return x.reshape(batch, seq, D_MODEL)


def kernel(x_prompt, x_sample, ffn_norm, ffn_w_gate, ffn_w_up, ffn_w_down, mix_norm, ev_w_in, ev_conv_w, mla_q_norm, mla_w_uq, mla_kv_norm, mla_w_ukv, ev_w_out, od_w_in, od_conv_w, od_conv_b, lru_w_a, lru_b_a, lru_w_x, lru_b_x, lru_lambda, swa_sink, od_w_out, final_norm):
    depth = ffn_norm.shape[0]
    ffn = [[(ffn_norm[l, s][None, :], ffn_w_gate[l, s].astype(BF16), ffn_w_up[l, s].astype(BF16),
             ffn_w_down[l, s].astype(BF16)) for s in range(2)] for l in range(depth)]
    even = [_even_weights(ev_w_in[j], ev_conv_w[j], mla_q_norm[j], mla_w_uq[j], mla_kv_norm[j], mla_w_ukv[j],
                          ev_w_out[j]) for j in range(ev_w_in.shape[0])]
    odd = [_odd_weights(od_w_in[j], od_conv_w[j], od_conv_b[j], lru_w_a[j], lru_b_a[j], lru_w_x[j], lru_b_x[j],
                        lru_lambda[j], swa_sink[j], od_w_out[j]) for j in range(od_w_in.shape[0])]
    tab = _rope_tables(max(x_prompt.shape[1], x_sample.shape[1]))
    y_prompt = _trunk(x_prompt, tab, ffn, mix_norm, even, odd, final_norm)
    y_sample = _trunk(x_sample, tab, ffn, mix_norm, even, odd, final_norm)
    return (y_prompt, y_sample)
```

```python
import functools
import math

import jax
import jax.numpy as jnp
from jax import lax
from jax.experimental import pallas as pl
from jax.experimental.pallas import tpu as pltpu

F32 = jnp.float32
BF16 = jnp.bfloat16

EPS = 1e-6
D_MODEL = 1024
D_FF = 2816
BLOCK = 128
LANES = 128
SUBLANES = 8

D_CONV = 512
MLA_HEADS = 8
QK_NOPE = 64
QK_ROPE = 32
V_DIM = 64
Q_RANK = 384
KV_RANK = 256
ROPE_THETA = 10000.0

D_RNN = 512
LRU_BLOCKS = 8
LRU_BW = D_RNN // LRU_BLOCKS
LRU_C = 8.0

SWA_HEADS = 8
SWA_KV_HEADS = 2
SWA_HD = 64
SWA_GROUP = SWA_HEADS // SWA_KV_HEADS
WINDOW = 128

TM = 512
MXU_TILE = 256
FF_SPLITS = (0, 4 * MXU_TILE, 8 * MXU_TILE, D_FF)
TM_FFN = 1024
TQ = 2048
FLASH_UNIT = 512
TK = 2048
FLASH_KEYS = 1024
FLASH_AHEAD = 2
FLASH_SLOTS = 4
FLASH_ROWS = 32
TS = 512
TW = 512
HALO = SUBLANES

VMEM_LIMIT = 56 * 1024 * 1024
FLASH_VMEM_LIMIT = VMEM_LIMIT

EV_QLAT = 3 * D_CONV
EV_KVLAT = EV_QLAT + Q_RANK
EV_KR = EV_KVLAT + KV_RANK
EV_IN_EXT = EV_KR + 2 * LANES
OD_Q = 2 * D_RNN
OD_K = OD_Q + SWA_HEADS * LANES
OD_V = OD_K + SWA_KV_HEADS * LANES
OD_IN_EXT = OD_V + SWA_KV_HEADS * LANES


def _const_spec(shape):
    zeros = (0,) * len(shape)
    return pl.BlockSpec(shape, lambda *_: zeros, pipeline_mode=pl.Buffered(1))


def _params(*sem):
    return pltpu.CompilerParams(dimension_semantics=sem, vmem_limit_bytes=VMEM_LIMIT)


def _rms(x, g):
    ms = jnp.mean(x * x, axis=-1, keepdims=True)
    return x * lax.rsqrt(ms + EPS) * g


def _dot(a, b):
    return jnp.dot(a, b, preferred_element_type=F32)


def _dot_nt(a, b):
    return lax.dot_general(a, b, (((1,), (1,)), ((), ())), preferred_element_type=F32)


def _swiglu_residual(x, g_ref, wg_ref, wu_ref, wd_ref):
    xn = _rms(x, g_ref[...]).astype(BF16)
    acc = None
    for lo, hi in zip(FF_SPLITS[:-1], FF_SPLITS[1:]):
        sl = slice(lo, hi)
        gate = _dot(xn, wg_ref[:, sl])
        up = _dot(xn, wu_ref[:, sl])
        h = (gate * jax.nn.sigmoid(gate) * up).astype(BF16)
        y = _dot(h, wd_ref[sl, :])
        acc = y if acc is None else acc + y
    return x + 0.5 * acc


def _ffn_body(x_ref, g_ref, wg_ref, wu_ref, wd_ref, *rest, final):
    o_ref = rest[-1]
    out = _swiglu_residual(x_ref[...], g_ref, wg_ref, wu_ref, wd_ref)
    if final:
        out = _rms(out, rest[0][...])
    o_ref[...] = out


def _ffn(x, g, wg, wu, wd, final_g=None):
    m = x.shape[0]
    row = pl.BlockSpec((TM_FFN, D_MODEL), lambda i: (i, 0))
    in_specs = [row, _const_spec((1, D_MODEL)), _const_spec((D_MODEL, D_FF)),
                _const_spec((D_MODEL, D_FF)), _const_spec((D_FF, D_MODEL))]
    args = [x, g, wg, wu, wd]
    if final_g is not None:
        in_specs.append(_const_spec((1, D_MODEL)))
        args.append(final_g)
    return pl.pallas_call(
        functools.partial(_ffn_body, final=final_g is not None),
        grid=(m // TM_FFN,), in_specs=in_specs, out_specs=row,
        out_shape=jax.ShapeDtypeStruct((m, D_MODEL), F32),
        compiler_params=_params("parallel"), name="ffn")(*args)


def _even_in_body(x_ref, g_ref, win_ref, qg_ref, kvg_ref, wqa_ref, wqb_ref, wuk_ref, wuv_ref, tab_ref,
                  bg_ref, u_ref, q_ref, k_ref, v_ref):
    xn = _rms(x_ref[...], g_ref[...]).astype(BF16)
    z = _dot(xn, win_ref[...])
    bg_ref[...] = z[:, :D_CONV]
    u_ref[...] = z[:, D_CONV:2 * D_CONV] * z[:, 2 * D_CONV:3 * D_CONV]
    qn = _rms(z[:, EV_QLAT:EV_KVLAT], qg_ref[...]).astype(BF16)
    kvn = _rms(z[:, EV_KVLAT:EV_KR], kvg_ref[...]).astype(BF16)
    cosk = tab_ref[:, 0:LANES]
    sink = tab_ref[:, LANES:2 * LANES]
    scale = (QK_NOPE + QK_ROPE) ** -0.5 * math.log2(math.e)
    nope = lax.broadcasted_iota(jnp.int32, (1, LANES), 1) < QK_NOPE
    cosq = scale * jnp.where(nope, 1.0, cosk)
    sinq = scale * sink
    kpe = z[:, EV_KR:EV_KR + LANES] * cosk + z[:, EV_KR + LANES:EV_KR + 2 * LANES] * sink
    qa = _dot(qn, wqa_ref[...])
    qb = _dot(qn, wqb_ref[...])
    kn = _dot(kvn, wuk_ref[...])
    lane = lax.broadcasted_iota(jnp.int32, (1, MLA_HEADS * LANES), 1)
    ones_lanes = jnp.where(lane % LANES >= V_DIM, 1.0, 0.0)
    v_ref[...] = (_dot(kvn, wuv_ref[...]) + ones_lanes).astype(BF16)
    for h in range(MLA_HEADS):
        sl = slice(h * LANES, (h + 1) * LANES)
        q_ref[:, sl] = (qa[:, sl] * cosq + qb[:, sl] * sinq).astype(BF16)
        k_ref[:, sl] = (kn[:, sl] + kpe).astype(BF16)


def _even_in(x, g, w, tab, seq):
    m = x.shape[0]
    hw = MLA_HEADS * LANES
    tiles_per_seq = seq // TM
    row = lambda n: pl.BlockSpec((TM, n), lambda i: (i, 0))
    in_specs = [row(D_MODEL), _const_spec((1, D_MODEL)), _const_spec((D_MODEL, EV_IN_EXT)),
                _const_spec((1, Q_RANK)), _const_spec((1, KV_RANK)),
                _const_spec((Q_RANK, hw)), _const_spec((Q_RANK, hw)),
                _const_spec((KV_RANK, hw)), _const_spec((KV_RANK, hw)),
                pl.BlockSpec((TM, 2 * LANES), lambda i: (i % tiles_per_seq, 0))]
    out_specs = [row(D_CONV), row(D_CONV), row(hw), row(hw), row(hw)]
    out_shape = [jax.ShapeDtypeStruct((m, D_CONV), F32), jax.ShapeDtypeStruct((m, D_CONV), F32),
                 jax.ShapeDtypeStruct((m, hw), BF16), jax.ShapeDtypeStruct((m, hw), BF16),
                 jax.ShapeDtypeStruct((m, hw), BF16)]
    return pl.pallas_call(
        _even_in_body, grid=(m // TM,), in_specs=in_specs, out_specs=out_specs, out_shape=out_shape,
        compiler_params=_params("parallel"), name="even_in")(
            x, g, w["win"], w["q_norm"], w["kv_norm"], w["wqa"], w["wqb"], w["wuk"], w["wuv"], tab)


def _flash_body(q_ref, k_ref, v_ref, o_ref, m_ref, acc_ref, s_buf, p_buf, *, n_kv):
    j = pl.program_id(2)

    @pl.when(j == 0)
    def _():
        m_ref[...] = jnp.full(m_ref.shape, -jnp.inf, F32)
        acc_ref[...] = jnp.zeros(acc_ref.shape, F32)

    units = [(h, r0) for h in range(MLA_HEADS) for r0 in range(0, TQ, FLASH_UNIT)]
    n_chunks = TK // FLASH_KEYS
    assert len(units) % FLASH_SLOTS == 0

    def scores(u, k0):
        h, r0 = units[u]
        sl = slice(h * LANES, (h + 1) * LANES)
        s_buf[u % FLASH_SLOTS] = _dot_nt(q_ref[r0:r0 + FLASH_UNIT, sl], k_ref[pl.ds(k0, FLASH_KEYS), sl])

    def chunk(c, carry):
        k0 = pl.multiple_of(c * FLASH_KEYS, FLASH_KEYS)
        k_next = pl.multiple_of(jnp.minimum(c + 1, n_chunks - 1) * FLASH_KEYS, FLASH_KEYS)
        for u, (h, r0) in enumerate(units):
            ahead = u + FLASH_AHEAD
            if ahead < len(units):
                scores(ahead, k0)
            else:
                scores(ahead - len(units), k_next)
            for r in range(0, FLASH_UNIT, FLASH_ROWS):
                rows = slice(r0 + r, r0 + r + FLASH_ROWS)
                s = s_buf[u % FLASH_SLOTS, r:r + FLASH_ROWS, :]
                m_prev = m_ref[h, rows, :]
                m_new = jnp.maximum(m_prev, jnp.max(s, axis=-1, keepdims=True))
                p_buf[u % 2, r:r + FLASH_ROWS, :] = jnp.exp2(s - m_new).astype(BF16)
                acc_ref[h, rows, :] = acc_ref[h, rows, :] * jnp.exp2(m_prev - m_new)
                m_ref[h, rows, :] = m_new
            acc_ref[h, r0:r0 + FLASH_UNIT, :] += _dot(p_buf[u % 2],
                                                      v_ref[pl.ds(k0, FLASH_KEYS), h * LANES:(h + 1) * LANES])
        return carry

    for u in range(FLASH_AHEAD):
        scores(u, 0)
    lax.fori_loop(0, n_chunks, chunk, 0)

    @pl.when(j == n_kv - 1)
    def _():
        low_half = lax.broadcasted_iota(jnp.int32, (TQ, LANES), 1) < V_DIM
        for pair in range(MLA_HEADS // 2):
            even = acc_ref[2 * pair]
            odd = acc_ref[2 * pair + 1]
            even_sw = pltpu.roll(even, V_DIM, 1)
            odd_sw = pltpu.roll(odd, V_DIM, 1)
            out = jnp.where(low_half, even / even_sw, odd_sw / odd)
            o_ref[:, pair * LANES:(pair + 1) * LANES] = out.astype(BF16)


def _flash(q, k, v, batch, seq):
    m = q.shape[0]
    hw = MLA_HEADS * LANES
    nq, nk = seq // TQ, seq // TK
    ow = MLA_HEADS * V_DIM
    return pl.pallas_call(
        functools.partial(_flash_body, n_kv=nk), grid=(batch, nq, nk),
        in_specs=[pl.BlockSpec((TQ, hw), lambda b, i, j: (b * nq + i, 0)),
                  pl.BlockSpec((TK, hw), lambda b, i, j: (b * nk + j, 0)),
                  pl.BlockSpec((TK, hw), lambda b, i, j: (b * nk + j, 0))],
        out_specs=pl.BlockSpec((TQ, ow), lambda b, i, j: (b * nq + i, 0)),
        out_shape=jax.ShapeDtypeStruct((m, ow), BF16),
        scratch_shapes=[pltpu.VMEM((MLA_HEADS, TQ, 1), F32), pltpu.VMEM((MLA_HEADS, TQ, LANES), F32),
                        pltpu.VMEM((FLASH_SLOTS, FLASH_UNIT, FLASH_KEYS), F32),
                        pltpu.VMEM((2, FLASH_UNIT, FLASH_KEYS), BF16)],
        compiler_params=pltpu.CompilerParams(dimension_semantics=("parallel", "parallel", "arbitrary"),
                                             vmem_limit_bytes=FLASH_VMEM_LIMIT), name="mla_flash")(q, k, v)


def _even_out_ffn_body(bg_ref, u_ref, up_ref, un_ref, cw_ref, yb_ref, wo_ref, x_ref, g_ref, wg_ref, wu_ref, wd_ref,
                       o_ref, *, tiles_per_seq):
    t = pl.program_id(0) % tiles_per_seq
    u = u_ref[...]
    row = lax.broadcasted_iota(jnp.int32, u.shape, 0)
    prev = jnp.where(t == 0, 0.0, up_ref[HALO - 1:HALO, :])
    nxt = jnp.where(t == tiles_per_seq - 1, 0.0, un_ref[0:1, :])
    um1 = jnp.where(row == 0, prev, pltpu.roll(u, 1, 0))
    up1 = jnp.where(row == TM - 1, nxt, pltpu.roll(u, TM - 1, 0))
    conv = um1 * cw_ref[0:1, :] + u * cw_ref[1:2, :] + up1 * cw_ref[2:3, :]
    ya = (bg_ref[...] * conv).astype(BF16)
    x = x_ref[...] + _dot(ya, wo_ref[0:D_CONV, :]) + _dot(yb_ref[...], wo_ref[D_CONV:, :])
    o_ref[...] = _swiglu_residual(x, g_ref, wg_ref, wu_ref, wd_ref)


def _halo_specs(tile, width, n_rows):
    per = tile // HALO
    last = n_rows // HALO - 1
    prev = pl.BlockSpec((HALO, width), lambda i: (jnp.maximum(i * per - 1, 0), 0))
    nxt = pl.BlockSpec((HALO, width), lambda i: (jnp.minimum((i + 1) * per, last), 0))
    return prev, nxt


def _even_out_ffn(bg, u, conv_w, yb, wo, x, g, wg, wu, wd, seq):
    m = x.shape[0]
    row = lambda n: pl.BlockSpec((TM, n), lambda i: (i, 0))
    prev, nxt = _halo_specs(TM, D_CONV, m)
    return pl.pallas_call(
        functools.partial(_even_out_ffn_body, tiles_per_seq=seq // TM), grid=(m // TM,),
        in_specs=[row(D_CONV), row(D_CONV), prev, nxt, _const_spec(conv_w.shape), row(MLA_HEADS * V_DIM),
                  _const_spec(wo.shape), row(D_MODEL), _const_spec((1, D_MODEL)), _const_spec((D_MODEL, D_FF)),
                  _const_spec((D_MODEL, D_FF)), _const_spec((D_FF, D_MODEL))],
        out_specs=row(D_MODEL), out_shape=jax.ShapeDtypeStruct((m, D_MODEL), F32),
        compiler_params=_params("parallel"), name="even_out_ffn")(bg, u, u, u, conv_w, yb, wo, x, g, wg, wu, wd)


def _odd_in_body(x_ref, xp_ref, xn_ref, g_ref, win_ref, cw_ref, cb_ref, xc_ref, gate_ref, q_ref, k_ref, v_ref, *,
                 tiles_per_seq):
    t = pl.program_id(0) % tiles_per_seq
    x_ext = jnp.concatenate([xp_ref[...], x_ref[...], xn_ref[...]], axis=0)
    z_ext = _dot(_rms(x_ext, g_ref[...]).astype(BF16), win_ref[...])
    z = z_ext[HALO:HALO + TM, :]
    xr = z_ext[:, :D_RNN]
    row = lax.broadcasted_iota(jnp.int32, xr.shape, 0)
    outside = ((row < HALO) & (t == 0)) | ((row >= HALO + TM) & (t == tiles_per_seq - 1))
    xr = jnp.where(outside, 0.0, xr)
    n_ext = TM + 2 * HALO
    xc = (pltpu.roll(xr, 2, 0) * cw_ref[0:1, :] + pltpu.roll(xr, 1, 0) * cw_ref[1:2, :] + xr * cw_ref[2:3, :]
          + pltpu.roll(xr, n_ext - 1, 0) * cw_ref[3:4, :])
    xc_ref[...] = xc[HALO:HALO + TM, :] + cb_ref[...]
    gate_ref[...] = z[:, D_RNN:OD_Q]
    q_ref[...] = (z[:, OD_Q:OD_K] * (SWA_HD ** -0.5)).astype(BF16)
    k_ref[...] = z[:, OD_K:OD_V].astype(BF16)
    lane = lax.broadcasted_iota(jnp.int32, (1, SWA_KV_HEADS * LANES), 1)
    v_ref[...] = (z[:, OD_V:] + jnp.where(lane % LANES >= SWA_HD, 1.0, 0.0)).astype(BF16)


def _odd_in(x, g, w, seq):
    m = x.shape[0]
    row = lambda n: pl.BlockSpec((TM, n), lambda i: (i, 0))
    prev, nxt = _halo_specs(TM, D_MODEL, m)
    widths = [D_RNN, D_RNN, SWA_HEADS * LANES, SWA_KV_HEADS * LANES, SWA_KV_HEADS * LANES]
    dtypes = [F32, F32, BF16, BF16, BF16]
    return pl.pallas_call(
        functools.partial(_odd_in_body, tiles_per_seq=seq // TM), grid=(m // TM,),
        in_specs=[row(D_MODEL), prev, nxt, _const_spec((1, D_MODEL)), _const_spec((D_MODEL, OD_IN_EXT)),
                  _const_spec(w["conv_w"].shape), _const_spec((1, D_RNN))],
        out_specs=[row(n) for n in widths],
        out_shape=[jax.ShapeDtypeStruct((m, n), d) for n, d in zip(widths, dtypes)],
        compiler_params=_params("parallel"), name="odd_in")(x, x, x, g, w["win"], w["conv_w"], w["conv_b"])


def _lru_body(x_ref, wa_ref, ba_ref, wx_ref, bx_ref, lam_ref, o_ref, a_s, u_s, c_s, carry_s, *, reverse):
    @pl.when(pl.program_id(1) == 0)
    def _():
        carry_s[...] = jnp.zeros(carry_s.shape, F32)

    xc = x_ref[...]
    xcb = xc.astype(BF16)
    tr = jnp.tanh(0.5 * (_dot(xcb, wa_ref[...]) + ba_ref[...]))
    ti = jnp.tanh(0.5 * (_dot(xcb, wx_ref[...]) + bx_ref[...]))
    nl = -lam_ref[...]
    softplus = jnp.maximum(nl, 0.0) + jnp.log1p(jnp.exp(-jnp.abs(nl)))
    log_a = (-0.5 * LRU_C * softplus) * (tr + 1.0)
    a = jnp.exp(log_a)
    th = jnp.tanh(log_a)
    u = jnp.sqrt(-0.5 * th / (1.0 - th)) * ((ti + 1.0) * xc)
    n_groups = TS // SUBLANES
    a = a.reshape(n_groups, SUBLANES, D_RNN)
    u = u.reshape(n_groups, SUBLANES, D_RNN)
    in_group = lax.broadcasted_iota(jnp.int32, a.shape, 1)
    for d in (1, 2, 4):
        if reverse:
            keep = in_group < SUBLANES - d
            shift = SUBLANES - d
        else:
            keep = in_group >= d
            shift = d
        a_sh = jnp.where(keep, pltpu.roll(a, shift, 1), 1.0)
        u_sh = jnp.where(keep, pltpu.roll(u, shift, 1), 0.0)
        u = a * u_sh + u
        a = a * a_sh
    a_s[...] = a.reshape(TS, D_RNN)
    u_s[...] = u.reshape(TS, D_RNN)
    c = carry_s[...]
    for g in (range(n_groups - 1, -1, -1) if reverse else range(n_groups)):
        c_s[g:g + 1, :] = c
        e = g * SUBLANES + (0 if reverse else SUBLANES - 1)
        c = a_s[e:e + 1, :] * c + u_s[e:e + 1, :]
    carry_s[...] = c
    for g in range(n_groups):
        rows = slice(g * SUBLANES, (g + 1) * SUBLANES)
        o_ref[rows, :] = u_s[rows, :] + a_s[rows, :] * c_s[g:g + 1, :]


def _lru(xc, w, d, batch, seq, reverse):
    m = xc.shape[0]
    n_tiles = seq // TS
    main = pl.BlockSpec((TS, D_RNN), lambda b, i: (b * n_tiles + ((n_tiles - 1 - i) if reverse else i), 0))
    vec = _const_spec((1, D_RNN))
    mat = _const_spec((D_RNN, D_RNN))
    big = pltpu.VMEM((TS, D_RNN), F32)
    return pl.pallas_call(
        functools.partial(_lru_body, reverse=reverse), grid=(batch, n_tiles),
        in_specs=[main, mat, vec, mat, vec, vec],
        out_specs=main, out_shape=jax.ShapeDtypeStruct((m, D_RNN), F32),
        scratch_shapes=[big, big, pltpu.VMEM((TS // SUBLANES, D_RNN), F32), pltpu.VMEM((1, D_RNN), F32)],
        compiler_params=_params("arbitrary", "arbitrary"), name="lru_bwd" if reverse else "lru_fwd")(
            xc, w["wa"][d], w["ba"][d], w["wx"][d], w["bx"][d], w["lam"][d])


def _odd_out_body(sink_ref, q_ref, k_ref, kp_ref, kn_ref, v_ref, vp_ref, vn_ref, gate_ref, hf_ref, hb_ref,
                  wo_ref, x_ref, o_ref, kbuf, vbuf, s_buf, p_buf, yd_s, *, tiles_per_seq, seq):
    t = pl.program_id(0) % tiles_per_seq
    kbuf[0:BLOCK, :] = kp_ref[...]
    kbuf[BLOCK:BLOCK + TW, :] = k_ref[...]
    kbuf[BLOCK + TW:, :] = kn_ref[...]
    vbuf[0:BLOCK, :] = vp_ref[...]
    vbuf[BLOCK:BLOCK + TW, :] = v_ref[...]
    vbuf[BLOCK + TW:, :] = vn_ref[...]

    qi = lax.broadcasted_iota(jnp.int32, (BLOCK, 3 * BLOCK), 0)
    ci = lax.broadcasted_iota(jnp.int32, (BLOCK, 3 * BLOCK), 1)
    rel = jnp.abs(ci - BLOCK - qi)
    relf = rel.astype(F32)
    in_window = rel <= WINDOW
    low_half = lax.broadcasted_iota(jnp.int32, (BLOCK, LANES), 1) < SWA_HD

    units = [(jb, g) for jb in range(TW // BLOCK) for g in range(SWA_KV_HEADS)]

    def scores(u):
        jb, g = units[u]
        qs = jnp.concatenate([q_ref[jb * BLOCK:(jb + 1) * BLOCK, h * LANES:(h + 1) * LANES]
                              for h in range(g * SWA_GROUP, (g + 1) * SWA_GROUP)], axis=0)
        s_buf[u % 2] = _dot_nt(qs, kbuf[jb * BLOCK:(jb + 3) * BLOCK, g * LANES:(g + 1) * LANES])

    scores(0)
    for u, (jb, g) in enumerate(units):
        if u + 1 < len(units):
            scores(u + 1)
        key_pos = t * TW + (jb - 1) * BLOCK + ci
        valid = in_window & (key_pos >= 0) & (key_pos < seq)
        sink_terms = []
        for hh in range(SWA_GROUP):
            h = g * SWA_GROUP + hh
            rows = slice(hh * BLOCK, (hh + 1) * BLOCK)
            s = jnp.where(valid, s_buf[u % 2, rows, :] - (2.0 ** -(h + 1)) * relf, -jnp.inf)
            sk = sink_ref[h]
            mx = jnp.maximum(jnp.max(s, axis=-1, keepdims=True), sk)
            p_buf[u % 2, rows, :] = jnp.exp(s - mx).astype(BF16)
            sink_terms.append(jnp.exp(sk - mx))
        o = _dot(p_buf[u % 2], vbuf[jb * BLOCK:(jb + 3) * BLOCK, g * LANES:(g + 1) * LANES])
        for pp in range(SWA_GROUP // 2):
            oe = o[2 * pp * BLOCK:(2 * pp + 1) * BLOCK, :]
            oo = o[(2 * pp + 1) * BLOCK:(2 * pp + 2) * BLOCK, :]
            ye = oe / (pltpu.roll(oe, SWA_HD, 1) + sink_terms[2 * pp])
            yo = pltpu.roll(oo, SWA_HD, 1) / (oo + sink_terms[2 * pp + 1])
            pair = (g * SWA_GROUP) // 2 + pp
            yd_s[jb * BLOCK:(jb + 1) * BLOCK, pair * LANES:(pair + 1) * LANES] = (
                jnp.where(low_half, ye, yo).astype(BF16))

    yc = (jax.nn.gelu(gate_ref[...]) * (hf_ref[...] + hb_ref[...])).astype(BF16)
    y = _dot(yc, wo_ref[0:D_RNN, :]) + _dot(yd_s[...], wo_ref[D_RNN:, :])
    o_ref[...] = x_ref[...] + y


def _odd_out(sink, q, k, v, gate, hf, hb, wo, x, seq):
    m = x.shape[0]
    kw = SWA_KV_HEADS * LANES
    vw = SWA_KV_HEADS * LANES
    unit = (SWA_GROUP * BLOCK, 3 * BLOCK)
    per = TW // BLOCK
    last = m // BLOCK - 1
    row = lambda n: pl.BlockSpec((TW, n), lambda i: (i, 0))
    prev = lambda n: pl.BlockSpec((BLOCK, n), lambda i: (jnp.maximum(i * per - 1, 0), 0))
    nxt = lambda n: pl.BlockSpec((BLOCK, n), lambda i: (jnp.minimum((i + 1) * per, last), 0))
    return pl.pallas_call(
        functools.partial(_odd_out_body, tiles_per_seq=seq // TW, seq=seq), grid=(m // TW,),
        in_specs=[pl.BlockSpec(memory_space=pltpu.SMEM),
                  row(SWA_HEADS * LANES), row(kw), prev(kw), nxt(kw), row(vw), prev(vw), nxt(vw),
                  row(D_RNN), row(D_RNN), row(D_RNN), _const_spec(wo.shape), row(D_MODEL)],
        out_specs=row(D_MODEL), out_shape=jax.ShapeDtypeStruct((m, D_MODEL), F32),
        scratch_shapes=[pltpu.VMEM((TW + 2 * BLOCK, kw), BF16), pltpu.VMEM((TW + 2 * BLOCK, vw), BF16),
                        pltpu.VMEM((2,) + unit, F32), pltpu.VMEM((2,) + unit, BF16),
                        pltpu.VMEM((TW, SWA_HEADS * SWA_HD), BF16)],
        compiler_params=_params("parallel"), name="odd_out")(
            sink, q, k, k, k, v, v, v, gate, hf, hb, wo, x)


def _rotate_half_cols(w):
    half = w.shape[-1] // 2
    return jnp.concatenate([-w[..., half:], w[..., :half]], axis=-1)


def _even_weights(w_in, conv_w, q_norm, w_uq, kv_norm, w_ukv, w_out):
    zc = lambda n: jnp.zeros((D_MODEL, n), F32)
    kr = w_in[:, EV_KR:]
    win = jnp.concatenate([w_in[:, :EV_KR], zc(QK_NOPE), kr, zc(LANES - QK_NOPE - QK_ROPE),
                           zc(QK_NOPE), _rotate_half_cols(kr), zc(LANES - QK_NOPE - QK_ROPE)], axis=1)
    wq = w_uq.reshape(Q_RANK, MLA_HEADS, QK_NOPE + QK_ROPE)
    nope, pe = wq[..., :QK_NOPE], wq[..., QK_NOPE:]
    zq = lambda n: jnp.zeros((Q_RANK, MLA_HEADS, n), F32)
    wqa = jnp.concatenate([nope, pe, zq(LANES - QK_NOPE - QK_ROPE)], -1).reshape(Q_RANK, MLA_HEADS * LANES)
    wqb = jnp.concatenate([zq(QK_NOPE), _rotate_half_cols(pe), zq(LANES - QK_NOPE - QK_ROPE)], -1)
    wqb = wqb.reshape(Q_RANK, MLA_HEADS * LANES)
    wkv = w_ukv.reshape(KV_RANK, MLA_HEADS, QK_NOPE + V_DIM)
    wuk = jnp.concatenate([wkv[..., :QK_NOPE], jnp.zeros((KV_RANK, MLA_HEADS, LANES - QK_NOPE), F32)], -1)
    wuv = jnp.concatenate([wkv[..., QK_NOPE:], jnp.zeros((KV_RANK, MLA_HEADS, LANES - V_DIM), F32)], -1)
    return dict(win=win.astype(BF16), conv_w=conv_w, q_norm=q_norm[None, :], kv_norm=kv_norm[None, :],
                wqa=wqa.astype(BF16), wqb=wqb.astype(BF16),
                wuk=wuk.reshape(KV_RANK, MLA_HEADS * LANES).astype(BF16),
                wuv=wuv.reshape(KV_RANK, MLA_HEADS * LANES).astype(BF16), wo=w_out.astype(BF16))


def _block_diag(w):
    eye = jnp.eye(LRU_BLOCKS, dtype=w.dtype)
    return jnp.einsum('ncd,nm->ncmd', w, eye).reshape(D_RNN, D_RNN)


def _odd_weights(w_in, conv_w, conv_b, w_a, b_a, w_x, b_x, lam, sink, w_out):
    qd = SWA_HEADS * SWA_HD
    kd = SWA_KV_HEADS * SWA_HD
    pad = lambda w, heads: jnp.concatenate(
        [w.reshape(D_MODEL, heads, SWA_HD), jnp.zeros((D_MODEL, heads, LANES - SWA_HD), F32)], -1
    ).reshape(D_MODEL, heads * LANES)
    wq = w_in[:, OD_Q:OD_Q + qd]
    wk = w_in[:, OD_Q + qd:OD_Q + qd + kd]
    wv = w_in[:, OD_Q + qd + kd:]
    win = jnp.concatenate([w_in[:, :OD_Q], pad(wq, SWA_HEADS), pad(wk, SWA_KV_HEADS), pad(wv, SWA_KV_HEADS)],
                          axis=1)
    return dict(win=win.astype(BF16), conv_w=conv_w, conv_b=conv_b[None, :],
                wa=[_block_diag(w_a[d]).astype(BF16) for d in range(2)], ba=[b_a[d][None, :] for d in range(2)],
                wx=[_block_diag(w_x[d]).astype(BF16) for d in range(2)], bx=[b_x[d][None, :] for d in range(2)],
                lam=[lam[d][None, :] for d in range(2)], sink=sink, wo=w_out.astype(BF16))


def _rope_tables(seq):
    half = QK_ROPE // 2
    inv = ROPE_THETA ** (-jnp.arange(half, dtype=F32) / half)
    ang = jnp.arange(seq).astype(F32)[:, None] * inv[None, :]
    z_lo = jnp.zeros((seq, QK_NOPE), F32)
    z_hi = jnp.zeros((seq, LANES - QK_NOPE - QK_ROPE), F32)
    cos = jnp.concatenate([z_lo, jnp.cos(ang), jnp.cos(ang), z_hi], 1)
    sin = jnp.concatenate([z_lo, jnp.sin(ang), jnp.sin(ang), z_hi], 1)
    return jnp.concatenate([cos, sin], 1)


def _trunk(x3, tab, ffn, mix_norm, even, odd, final_norm):
    batch, seq, _ = x3.shape
    x = x3.reshape(batch * seq, D_MODEL)
    depth = len(ffn)
    for l in range(depth):
        x = _ffn(x, *ffn[l][0])
        g = mix_norm[l][None, :]
        if l % 2 == 0:
            w = even[l // 2]
            bg, u, q, k, v = _even_in(x, g, w, tab, seq)
            yb = _flash(q, k, v, batch, seq)
            assert l != depth - 1
            x = _even_out_ffn(bg, u, w["conv_w"], yb, w["wo"], x, *ffn[l][1], seq)
        else:
            w = odd[l // 2]
            xc, gate, q, k, v = _odd_in(x, g, w, seq)
            hf = _lru(xc, w, 0, batch, seq, reverse=False)
            hb = _lru(xc, w, 1, batch, seq, reverse=True)
            x = _odd_out(w["sink"], q, k, v, gate, hf, hb, w["wo"], x, seq)
            x = _ffn(x, *ffn[l][1], final_g=final_norm[None, :] if l == depth - 1 else None)
    return x.reshape(batch, seq, D_MODEL)


def kernel(x_prompt, x_sample, ffn_norm, ffn_w_gate, ffn_w_up, ffn_w_down, mix_norm, ev_w_in, ev_conv_w, mla_q_norm, mla_w_uq, mla_kv_norm, mla_w_ukv, ev_w_out, od_w_in, od_conv_w, od_conv_b, lru_w_a, lru_b_a, lru_w_x, lru_b_x, lru_lambda, swa_sink, od_w_out, final_norm):
    depth = ffn_norm.shape[0]
    ffn = [[(ffn_norm[l, s][None, :], ffn_w_gate[l, s].astype(BF16), ffn_w_up[l, s].astype(BF16),
             ffn_w_down[l, s].astype(BF16)) for s in range(2)] for l in range(depth)]
    even = [_even_weights(ev_w_in[j], ev_conv_w[j], mla_q_norm[j], mla_w_uq[j], mla_kv_norm[j], mla_w_ukv[j],
                          ev_w_out[j]) for j in range(ev_w_in.shape[0])]
    odd = [_odd_weights(od_w_in[j], od_conv_w[j], od_conv_b[j], lru_w_a[j], lru_b_a[j], lru_w_x[j], lru_b_x[j],
                        lru_lambda[j], swa_sink[j], od_w_out[j]) for j in range(od_w_in.shape[0])]
    tab = _rope_tables(max(x_prompt.shape[1], x_sample.shape[1]))
    y_prompt = _trunk(x_prompt, tab, ffn, mix_norm, even, odd, final_norm)
    y_sample = _trunk(x_sample, tab, ffn, mix_norm, even, odd, final_norm)
    return (y_prompt, y_sample)
```
